```python
import math
import jax, jax.numpy as jnp
from jax import lax
import numpy as np

D_MODEL = 1024
BATCH = 8
SEQ = 4096
DEPTH = 1
DEC_BATCH = 128
DEC_SEQ = 8
PAST_LEN = 8192
PAGE_SIZE = 128

D_SSM = D_MODEL // 2
SSM_GROUP = 16
N_SSM_GROUPS = D_SSM // SSM_GROUP
SSM_STATE = 64
D_ATT = D_MODEL - D_SSM
HEAD_DIM = 64
N_HEADS = D_ATT // HEAD_DIM
N_KV_HEADS = 4
Q_PER_KV = N_HEADS // N_KV_HEADS
N_IDX_HEADS = 4
IDX_DIM = 64
TOPK_KEYS = 256
Q_BLOCK = 128
N_EXPERTS = 64
TOP_K_EXPERTS = 6
D_EXPERT = 256
ROUTED_SCALE = 2.5
MOE_BLOCK = 128
ALPHA = (2 * DEPTH) ** 0.25
BETA = (8 * DEPTH) ** -0.25
LN_EPS = 1e-5
DT_MIN = 1e-3
DT_MAX = 1e-1
IN_SIZES = (D_SSM, N_HEADS * HEAD_DIM, N_KV_HEADS * HEAD_DIM, N_KV_HEADS * HEAD_DIM,
            N_IDX_HEADS * IDX_DIM, IDX_DIM, N_IDX_HEADS)

kernel_name = 'hymba_s5_dsa_moe_decode_step'


def _layer_norm(x, g, b):
    xf = x.astype(jnp.float32)
    mu = jnp.mean(xf, -1, keepdims=True)
    var = jnp.mean(jnp.square(xf - mu), -1, keepdims=True)
    return ((xf - mu) * lax.rsqrt(var + LN_EPS)).astype(x.dtype) * g + b


def _rms_norm(x, g):
    xf = x.astype(jnp.float32)
    return (xf * lax.rsqrt(jnp.mean(jnp.square(xf), -1, keepdims=True) + LN_EPS)).astype(x.dtype) * g


def _swiglu(x, w1, w3, w2):
    return (jax.nn.silu(x @ w1) * (x @ w3)) @ w2


def _input_projection(h, w_in):
    b, t, _ = h.shape
    cuts = np.cumsum(IN_SIZES)[:-1].tolist()
    u, q, k, v, qi, ki, wi = jnp.split(h @ w_in, cuts, axis=-1)
    return (u,
            q.reshape(b, t, N_HEADS, HEAD_DIM),
            k.reshape(b, t, N_KV_HEADS, HEAD_DIM),
            v.reshape(b, t, N_KV_HEADS, HEAD_DIM),
            qi.reshape(b, t, N_IDX_HEADS, IDX_DIM),
            ki,
            wi)


def _ssm_discretise(a_re, a_im, log_dt, b_re, b_im):
    f32 = jnp.float32
    ar, ai = a_re.astype(f32), a_im.astype(f32)
    dt = jnp.exp(log_dt.astype(f32))[:, None]
    mag = jnp.exp(dt * ar)
    abr, abi = mag * jnp.cos(dt * ai), mag * jnp.sin(dt * ai)
    den = ar * ar + ai * ai
    nr, ni = abr - 1.0, abi
    fr = (nr * ar + ni * ai) / den
    fi = (ni * ar - nr * ai) / den
    br, bi = b_re.astype(f32), b_im.astype(f32)
    bbr = fr[..., None] * br - fi[..., None] * bi
    bbi = fr[..., None] * bi + fi[..., None] * br
    return abr, abi, bbr, bbi


def _ssm_combine(e1, e2):
    a1r, a1i, b1r, b1i = e1
    a2r, a2i, b2r, b2i = e2
    return (a2r * a1r - a2i * a1i,
            a2r * a1i + a2i * a1r,
            a2r * b1r - a2i * b1i + b2r,
            a2r * b1i + a2i * b1r + b2i)


def _s5_group(u, h0_re, h0_im, abr, abi, bbr, bbi, c_re, c_im, d_skip, w_glu, b_glu):
    b, t, _ = u.shape
    f32 = jnp.float32
    uf = u.astype(f32).reshape(b, t, N_SSM_GROUPS, SSM_GROUP)
    bu_r = jnp.einsum('btgi,gpi->btgp', uf, bbr)
    bu_i = jnp.einsum('btgi,gpi->btgp', uf, bbi)
    a_r = jnp.broadcast_to(abr, bu_r.shape)
    a_i = jnp.broadcast_to(abi, bu_r.shape)
    pr, pi_, hr, hi = lax.associative_scan(_ssm_combine, (a_r, a_i, bu_r, bu_i), axis=1)
    if h0_re is not None:
        s_r = h0_re.astype(f32)[:, None]
        s_i = h0_im.astype(f32)[:, None]
        hr = hr + pr * s_r - pi_ * s_i
        hi = hi + pr * s_i + pi_ * s_r
    y = (jnp.einsum('btgp,gip->btgi', hr, c_re.astype(f32))
         - jnp.einsum('btgp,gip->btgi', hi, c_im.astype(f32)))
    y = (y + d_skip.astype(f32).reshape(N_SSM_GROUPS, SSM_GROUP) * uf).reshape(b, t, D_SSM)
    g = jax.nn.gelu(y.astype(u.dtype))
    out = g * jax.nn.sigmoid(g @ w_glu + b_glu)
    return out, hr[:, -1], hi[:, -1]


def _index_scores(qi, wi, ki):
    dots = jnp.einsum('bthd,bsd->bths', qi, ki).astype(jnp.float32) * IDX_DIM ** -0.5
    return jnp.einsum('bths,bth->bts', jax.nn.relu(dots), wi.astype(jnp.float32)) * N_IDX_HEADS ** -0.5


def _attend_selected(q, k_sel, v_sel, valid):
    b, t = q.shape[:2]
    qg = q.reshape(b, t, N_KV_HEADS, Q_PER_KV, HEAD_DIM)
    s = jnp.einsum('btngd,btknd->btngk', qg, k_sel).astype(jnp.float32) * HEAD_DIM ** -0.5
    s = jnp.where(valid[:, :, None, None, :], s, -jnp.inf)
    pr = jax.nn.softmax(s, axis=-1).astype(v_sel.dtype)
    o = jnp.einsum('btngk,btknd->btngd', pr, v_sel)
    return o.reshape(b, t, D_ATT)


def _dsa_prompt(q, k, v, qi, ki, wi):
    b, s_len = q.shape[:2]
    n_sel = min(TOPK_KEYS, s_len // 4)
    n_blocks = s_len // Q_BLOCK
    bidx = jnp.arange(b)[:, None, None]
    key_pos = jnp.arange(s_len)

    def block(i):
        t0 = i * Q_BLOCK
        qb = lax.dynamic_slice_in_dim(q, t0, Q_BLOCK, axis=1)
        qib = lax.dynamic_slice_in_dim(qi, t0, Q_BLOCK, axis=1)
        wib = lax.dynamic_slice_in_dim(wi, t0, Q_BLOCK, axis=1)
        tpos = t0 + jnp.arange(Q_BLOCK)
        sc = _index_scores(qib, wib, ki)
        sc = jnp.where((key_pos[None, :] <= tpos[:, None])[None], sc, -jnp.inf)
        _, idx = lax.top_k(sc, n_sel)
        valid = idx <= tpos[None, :, None]
        return _attend_selected(qb, k[bidx, idx], v[bidx, idx], valid)

    out = lax.map(block, jnp.arange(n_blocks))
    return jnp.transpose(out, (1, 0, 2, 3)).reshape(b, s_len, D_ATT)


def _dsa_sample(q, k, v, qi, ki, wi, cache_k, cache_v, cache_kidx, page_table):
    bd, t = q.shape[:2]
    n_pages = page_table.shape[1]
    past = n_pages * PAGE_SIZE
    n_keys = past + t
    n_sel = min(TOPK_KEYS, n_keys // 4)
    ki_past = cache_kidx[page_table].reshape(bd, past, IDX_DIM)
    ki_all = jnp.concatenate([ki_past, ki], axis=1)
    tpos = past + jnp.arange(t)
    sc = _index_scores(qi, wi, ki_all)
    sc = jnp.where((jnp.arange(n_keys)[None, :] <= tpos[:, None])[None], sc, -jnp.inf)
    _, idx = lax.top_k(sc, n_sel)
    valid = idx <= tpos[None, :, None]
    bidx = jnp.arange(bd)[:, None, None]
    pidx = jnp.minimum(idx, past - 1)
    phys = page_table[bidx, pidx // PAGE_SIZE]
    off = pidx % PAGE_SIZE
    nidx = jnp.clip(idx - past, 0, t - 1)
    in_past = (idx < past)[..., None, None]
    k_sel = jnp.where(in_past, cache_k[phys, off], k[bidx, nidx])
    v_sel = jnp.where(in_past, cache_v[phys, off], v[bidx, nidx])
    return _attend_selected(q, k_sel, v_sel, valid)


def _moe_ffn(h, w_router, b_router, w_e1, w_e3, w_e2, w_s1, w_s3, w_s2):
    n_tok, d = h.shape
    aff = jax.nn.sigmoid((h @ w_router).astype(jnp.float32))
    _, eidx = lax.top_k(aff + b_router.astype(jnp.float32), TOP_K_EXPERTS)
    gsel = jnp.take_along_axis(aff, eidx, axis=-1)
    gates = gsel / jnp.sum(gsel, -1, keepdims=True) * ROUTED_SCALE
    n_assign = n_tok * TOP_K_EXPERTS
    flat_e = eidx.reshape(-1)
    flat_t = jnp.repeat(jnp.arange(n_tok, dtype=jnp.int32), TOP_K_EXPERTS)
    flat_g = gates.reshape(-1)
    order = jnp.argsort(flat_e)
    se, st, sg = flat_e[order], flat_t[order], flat_g[order]
    counts = jnp.bincount(flat_e, length=N_EXPERTS)
    starts = jnp.cumsum(counts) - counts
    padded = (counts + MOE_BLOCK - 1) // MOE_BLOCK * MOE_BLOCK
    pends = jnp.cumsum(padded)
    pstarts = pends - padded
    dest = pstarts[se] + jnp.arange(n_assign) - starts[se]
    n_blocks = -(-n_assign // MOE_BLOCK) + N_EXPERTS
    n_slots = n_blocks * MOE_BLOCK
    slot_tok = jnp.full((n_slots,), n_tok, jnp.int32).at[dest].set(st)
    slot_gate = jnp.zeros((n_slots,), jnp.float32).at[dest].set(sg)
    block_e = jnp.minimum(jnp.searchsorted(pends, jnp.arange(n_blocks) * MOE_BLOCK, side='right'),
                          N_EXPERTS - 1)
    h_pad = jnp.concatenate([h, jnp.zeros((1, d), h.dtype)], axis=0)

    def expert_block(args):
        tok, e = args
        return _swiglu(h_pad[tok], w_e1[e], w_e3[e], w_e2[e])

    y = lax.map(expert_block, (slot_tok.reshape(n_blocks, MOE_BLOCK), block_e))
    y = y.reshape(n_slots, d) * slot_gate[:, None].astype(h.dtype)
    routed = jax.ops.segment_sum(y, slot_tok, num_segments=n_tok + 1)[:n_tok]
    return routed + _swiglu(h, w_s1, w_s3, w_s2)


def _decoder_layer(x, c, p, past=None):
    b, t, _ = x.shape
    mod = jax.nn.silu(c) @ p['w_ada'] + p['b_ada']
    sh1, sc1, g1, sh2, sc2, g2 = jnp.split(mod[:, None, :], 6, axis=-1)
    h = x * (1 + sc1) + sh1
    u, q, k, v, qi, ki, wi = _input_projection(h, p['w_in'])
    abr, abi, bbr, bbi = _ssm_discretise(p['ssm_a_re'], p['ssm_a_im'], p['ssm_log_dt'],
                                         p['ssm_b_re'], p['ssm_b_im'])
    if past is None:
        y_ssm, s_re, s_im = _s5_group(u, None, None, abr, abi, bbr, bbi, p['ssm_c_re'], p['ssm_c_im'],
                                      p['ssm_d'], p['w_glu'], p['b_glu'])
        y_att = _dsa_prompt(q, k, v, qi, ki, wi)
    else:
        y_ssm, s_re, s_im = _s5_group(u, past['ssm_re'], past['ssm_im'], abr, abi, bbr, bbi,
                                      p['ssm_c_re'], p['ssm_c_im'], p['ssm_d'], p['w_glu'], p['b_glu'])
        y_att = _dsa_sample(q, k, v, qi, ki, wi, past['k'], past['v'], past['kidx'], past['page_table'])
    mixed = jnp.concatenate([_rms_norm(y_ssm, p['g_ssm_out']), _rms_norm(y_att, p['g_att_out'])],
                            axis=-1) @ p['w_out']
    x = _layer_norm(ALPHA * x + g1 * mixed, p['ln1_g'], p['ln1_b'])
    h = x * (1 + sc2) + sh2
    ffn = _moe_ffn(h.reshape(b * t, D_MODEL), p['w_router'], p['b_router'], p['w_e1'], p['w_e3'],
                   p['w_e2'], p['w_s1'], p['w_s3'], p['w_s2']).reshape(b, t, D_MODEL)
    x = _layer_norm(ALPHA * x + g2 * ffn, p['ln2_g'], p['ln2_b'])
    return x, (k, v, ki, s_re, s_im)


def setup_inputs(seed: int = 0) -> dict:
    key = jax.random.key(seed)
    keys = list(jax.random.split(key, 64))
    f32 = jnp.float32

    def nrm(shape, scale=1.0):
        return jax.random.normal(keys.pop(), shape, f32) * scale

    n_pages = PAST_LEN // PAGE_SIZE
    n_pool = (DEC_BATCH * n_pages * 5) // 4
    L, D, G, P = DEPTH, D_MODEL, N_SSM_GROUPS, SSM_STATE
    inv = D ** -0.5
    x_prompt = nrm((BATCH, SEQ, D))
    x_sample = nrm((DEC_BATCH, DEC_SEQ, D))
    c_prompt = nrm((BATCH, D))
    c_sample = nrm((DEC_BATCH, D))
    cache_k = nrm((L, n_pool, PAGE_SIZE, N_KV_HEADS, HEAD_DIM))
    cache_v = nrm((L, n_pool, PAGE_SIZE, N_KV_HEADS, HEAD_DIM))
    cache_kidx = nrm((L, n_pool, PAGE_SIZE, IDX_DIM))
    state_ssm_re = nrm((L, DEC_BATCH, G, P), 0.3)
    state_ssm_im = nrm((L, DEC_BATCH, G, P), 0.3)
    page_table = jax.random.permutation(keys.pop(), n_pool)[:DEC_BATCH * n_pages].reshape(
        DEC_BATCH, n_pages).astype(jnp.int32)
    w_ada = nrm((L, D, 6 * D), 0.5 * inv)
    b_ada = nrm((L, 6 * D), 0.01)
    w_in = jnp.concatenate([nrm((L, D, D_SSM), inv),
                            nrm((L, D, N_HEADS * HEAD_DIM), inv),
                            nrm((L, D, N_KV_HEADS * HEAD_DIM), inv),
                            nrm((L, D, N_KV_HEADS * HEAD_DIM), inv * BETA),
                            nrm((L, D, N_IDX_HEADS * IDX_DIM), inv),
                            nrm((L, D, IDX_DIM), inv),
                            nrm((L, D, N_IDX_HEADS), inv)], axis=-1)
    ssm_a_re = -0.5 + nrm((L, G, P), 0.01)
    ssm_a_im = jnp.pi * jnp.arange(P, dtype=f32) + nrm((L, G, P), 0.01)
    ssm_log_dt = jax.random.uniform(keys.pop(), (L, G), f32, math.log(DT_MIN), math.log(DT_MAX))
    ssm_b_re = nrm((L, G, P, SSM_GROUP), (2 * SSM_GROUP) ** -0.5)
    ssm_b_im = nrm((L, G, P, SSM_GROUP), (2 * SSM_GROUP) ** -0.5)
    ssm_c_re = nrm((L, G, SSM_GROUP, P), (2 * P) ** -0.5)
    ssm_c_im = nrm((L, G, SSM_GROUP, P), (2 * P) ** -0.5)
    ssm_d = nrm((L, D_SSM))
    w_glu = nrm((L, D_SSM, D_SSM), D_SSM ** -0.5)
    b_glu = nrm((L, D_SSM), 0.01)
    g_ssm_out = 1.0 + nrm((L, D_SSM), 0.01)
    g_att_out = 1.0 + nrm((L, D_ATT), 0.01)
    w_out = nrm((L, D, D), inv * BETA)
    ln1_g = 1.0 + nrm((L, D), 0.01)
    ln1_b = nrm((L, D), 0.01)
    w_router = nrm((L, D, N_EXPERTS), inv)
    b_router = nrm((L, N_EXPERTS), 0.01)
    w_e1 = nrm((L, N_EXPERTS, D, D_EXPERT), inv)
    w_e3 = nrm((L, N_EXPERTS, D, D_EXPERT), inv)
    w_e2 = nrm((L, N_EXPERTS, D_EXPERT, D), D_EXPERT ** -0.5 * BETA)
    w_s1 = nrm((L, D, D_EXPERT), inv)
    w_s3 = nrm((L, D, D_EXPERT), inv)
    w_s2 = nrm((L, D_EXPERT, D), D_EXPERT ** -0.5 * BETA)
    ln2_g = 1.0 + nrm((L, D), 0.01)
    ln2_b = nrm((L, D), 0.01)
    return {'x_prompt': x_prompt, 'x_sample': x_sample, 'c_prompt': c_prompt, 'c_sample': c_sample,
            'cache_k': cache_k, 'cache_v': cache_v, 'cache_kidx': cache_kidx,
            'state_ssm_re': state_ssm_re, 'state_ssm_im': state_ssm_im, 'page_table': page_table,
            'w_ada': w_ada, 'b_ada': b_ada, 'w_in': w_in,
            'ssm_a_re': ssm_a_re, 'ssm_a_im': ssm_a_im, 'ssm_log_dt': ssm_log_dt,
            'ssm_b_re': ssm_b_re, 'ssm_b_im': ssm_b_im, 'ssm_c_re': ssm_c_re, 'ssm_c_im': ssm_c_im,
            'ssm_d': ssm_d, 'w_glu': w_glu, 'b_glu': b_glu, 'g_ssm_out': g_ssm_out, 'g_att_out': g_att_out,
            'w_out': w_out, 'ln1_g': ln1_g, 'ln1_b': ln1_b, 'w_router': w_router, 'b_router': b_router,
            'w_e1': w_e1, 'w_e3': w_e3, 'w_e2': w_e2, 'w_s1': w_s1, 'w_s3': w_s3, 'w_s2': w_s2,
            'ln2_g': ln2_g, 'ln2_b': ln2_b}


def reference(x_prompt, x_sample, c_prompt, c_sample, cache_k, cache_v, cache_kidx, state_ssm_re,
              state_ssm_im, page_table, w_ada, b_ada, w_in, ssm_a_re, ssm_a_im, ssm_log_dt, ssm_b_re,
              ssm_b_im, ssm_c_re, ssm_c_im, ssm_d, w_glu, b_glu, g_ssm_out, g_att_out, w_out, ln1_g,
              ln1_b, w_router, b_router, w_e1, w_e3, w_e2, w_s1, w_s3, w_s2, ln2_g, ln2_b):
    weights = {'w_ada': w_ada, 'b_ada': b_ada, 'w_in': w_in, 'ssm_a_re': ssm_a_re, 'ssm_a_im': ssm_a_im,
               'ssm_log_dt': ssm_log_dt, 'ssm_b_re': ssm_b_re, 'ssm_b_im': ssm_b_im,
               'ssm_c_re': ssm_c_re, 'ssm_c_im': ssm_c_im, 'ssm_d': ssm_d, 'w_glu': w_glu,
               'b_glu': b_glu, 'g_ssm_out': g_ssm_out, 'g_att_out': g_att_out, 'w_out': w_out,
               'ln1_g': ln1_g, 'ln1_b': ln1_b, 'w_router': w_router, 'b_router': b_router,
               'w_e1': w_e1, 'w_e3': w_e3, 'w_e2': w_e2, 'w_s1': w_s1, 'w_s3': w_s3, 'w_s2': w_s2,
               'ln2_g': ln2_g, 'ln2_b': ln2_b}
    xp, xs = x_prompt, x_sample
    new_p = [[], [], [], [], []]
    new_s = [[], [], [], [], []]
    for l in range(DEPTH):
        p = {name: arr[l] for name, arr in weights.items()}
        xp, st_p = _decoder_layer(xp, c_prompt, p)
        past = {'k': cache_k[l], 'v': cache_v[l], 'kidx': cache_kidx[l],
                'ssm_re': state_ssm_re[l], 'ssm_im': state_ssm_im[l], 'page_table': page_table}
        xs, st_s = _decoder_layer(xs, c_sample, p, past)
        for lst, a in zip(new_p, st_p):
            lst.append(a)
        for lst, a in zip(new_s, st_s):
            lst.append(a)
    k_p, v_p, ki_p, re_p, im_p = [jnp.stack(a, axis=0) for a in new_p]
    k_s, v_s, ki_s, re_s, im_s = [jnp.stack(a, axis=0) for a in new_s]
    return (xp, xs, k_p, v_p, ki_p, re_p, im_p, k_s, v_s, ki_s, re_s, im_s)
```

```python
import functools
import math

import jax
import jax.numpy as jnp
import numpy as np
from jax import lax
from jax.experimental import pallas as pl
from jax.experimental.pallas import tpu as pltpu

D_MODEL = 1024
D_SSM = 512
SSM_GROUP = 16
N_SSM_GROUPS = 32
SSM_STATE = 64
D_ATT = 512
HEAD_DIM = 64
N_HEADS = 8
N_KV_HEADS = 4
N_IDX_HEADS = 4
IDX_DIM = 64
TOPK_KEYS = 256
N_EXPERTS = 64
TOP_K_EXPERTS = 6
D_EXPERT = 256
ROUTED_SCALE = 2.5
PAGE_SIZE = 128
ALPHA = 2.0 ** 0.25
LN_EPS = 1e-5

LANES = 128
VMEM_LIMIT = 56 * 1024 * 1024
NEG = -1e30
INT_MIN = -2 ** 31

F32 = jnp.float32
BF16 = jnp.bfloat16
HI = lax.Precision.HIGHEST


def _cparams(*sem):
    return pltpu.CompilerParams(dimension_semantics=sem, vmem_limit_bytes=VMEM_LIMIT)


def _dot(a, b):
    return jnp.dot(a, b, preferred_element_type=F32)


def _dot_nt(a, b):
    return lax.dot_general(a, b, (((1,), (1,)), ((), ())), preferred_element_type=F32)


def _ada_kernel(c_ref, w_ref, b_ref, o_ref):
    c = c_ref[...]
    s = (c * jax.nn.sigmoid(c)).astype(BF16)
    o_ref[...] = _dot(s, w_ref[...].astype(BF16)) + b_ref[...]


def _ada_mod(c, w_ada, b_ada):
    n, d = c.shape
    dn = w_ada.shape[1]
    tn = 1024
    return pl.pallas_call(
        _ada_kernel,
        grid=(dn // tn,),
        in_specs=[pl.BlockSpec((n, d), lambda j: (0, 0)),
                  pl.BlockSpec((d, tn), lambda j: (0, j)),
                  pl.BlockSpec((1, tn), lambda j: (0, j))],
        out_specs=pl.BlockSpec((n, tn), lambda j: (0, j)),
        out_shape=jax.ShapeDtypeStruct((n, dn), F32),
        compiler_params=_cparams("arbitrary"),
        name="ada_mod",
    )(c, w_ada, b_ada.reshape(1, dn))


_C_U, _C_Q, _C_K, _C_V, _C_QI, _C_KI, _C_WI = 0, 512, 1024, 1280, 1536, 1792, 1920
_C_KD, _C_VD, _C_KID, _C_END = 2048, 2560, 3072, 3200


def _dup_heads(w, n_heads, hd):
    d = w.shape[0]
    w = w.reshape(d, n_heads, 1, hd)
    return jnp.broadcast_to(w, (d, n_heads, 2, hd)).reshape(d, n_heads * 2 * hd)


def _widen_w_in(w_in):
    d = w_in.shape[0]
    cuts = np.cumsum([D_SSM, N_HEADS * HEAD_DIM, N_KV_HEADS * HEAD_DIM, N_KV_HEADS * HEAD_DIM,
                      N_IDX_HEADS * IDX_DIM, IDX_DIM])
    wu, wq, wk, wv, wqi, wki, wwi = jnp.split(w_in, cuts.tolist(), axis=1)
    z = lambda n: jnp.zeros((d, n), w_in.dtype)
    cols = [wu, wq * HEAD_DIM ** -0.5, wk, wv, wqi * IDX_DIM ** -0.5,
            wki, z(64), wwi * N_IDX_HEADS ** -0.5, z(124),
            _dup_heads(wk, N_KV_HEADS, HEAD_DIM), _dup_heads(wv, N_KV_HEADS, HEAD_DIM),
            _dup_heads(wki, 1, IDX_DIM)]
    return jnp.concatenate(cols, axis=1).astype(BF16)


def _inproj_kernel(x_ref, sc_ref, sh_ref, w_ref,
                   u_ref, q_ref, k_ref, v_ref, qi_ref, ki_ref, wi_ref, kd_ref, vd_ref, kid_ref):
    h = (x_ref[...] * (1.0 + sc_ref[...]) + sh_ref[...]).astype(BF16)
    r = _dot(h, w_ref[...])
    u_ref[...] = r[:, _C_U:_C_Q]
    q_ref[...] = r[:, _C_Q:_C_K].astype(BF16)
    k_ref[...] = r[:, _C_K:_C_V]
    v_ref[...] = r[:, _C_V:_C_QI]
    qi_ref[...] = r[:, _C_QI:_C_KI].astype(BF16)
    ki_ref[...] = r[:, _C_KI:_C_KI + IDX_DIM]
    wi_ref[...] = r[:, _C_WI:_C_WI + N_IDX_HEADS]
    kd_ref[...] = r[:, _C_KD:_C_VD].astype(BF16)
    vd_ref[...] = r[:, _C_VD:_C_KID].astype(BF16)
    kid_ref[...] = r[:, _C_KID:_C_END].astype(BF16)


def _mod_spec(mod, tm, rows_per_mod):
    if rows_per_mod is None:
        return pl.BlockSpec((tm, D_MODEL), lambda i: (i, 0))
    tiles = rows_per_mod // tm
    return pl.BlockSpec((None, 1, D_MODEL), lambda i: (i // tiles, 0, 0))


def _in_projection(x, sc, sh, w_wide, rows_per_mod, tm):
    t = x.shape[0]
    widths = [(D_SSM, F32), (512, BF16), (256, F32), (256, F32), (256, BF16), (IDX_DIM, F32),
              (N_IDX_HEADS, F32), (512, BF16), (512, BF16), (128, BF16)]
    ms = _mod_spec(sc, tm, rows_per_mod)
    return pl.pallas_call(
        _inproj_kernel,
        grid=(t // tm,),
        in_specs=[pl.BlockSpec((tm, D_MODEL), lambda i: (i, 0)), ms, ms,
                  pl.BlockSpec(w_wide.shape, lambda i: (0, 0))],
        out_specs=[pl.BlockSpec((tm, w), lambda i: (i, 0)) for w, _ in widths],
        out_shape=[jax.ShapeDtypeStruct((t, w), dt) for w, dt in widths],
        compiler_params=_cparams("arbitrary"),
        name="in_projection",
    )(x, sc, sh, w_wide)


def _ssm_matrices(a_re, a_im, log_dt, b_re, b_im, c_re, c_im, d_skip, chunk):
    g, p = a_re.shape
    L = chunk
    dt = jnp.exp(log_dt)[:, None]
    mag = jnp.exp(dt * a_re)
    abr, abi = mag * jnp.cos(dt * a_im), mag * jnp.sin(dt * a_im)
    den = a_re * a_re + a_im * a_im
    nr, ni = abr - 1.0, abi
    fr = (nr * a_re + ni * a_im) / den
    fi = (ni * a_re - nr * a_im) / den
    bbr = fr[..., None] * b_re - fi[..., None] * b_im
    bbi = fr[..., None] * b_im + fi[..., None] * b_re
    dd = jnp.arange(L + 1, dtype=F32)[:, None, None]
    pm = jnp.exp(dd * (dt * a_re)[None])
    pw_r = pm * jnp.cos(dd * (dt * a_im)[None])
    pw_i = pm * jnp.sin(dd * (dt * a_im)[None])
    ca_r = c_re[None] * pw_r[:, :, None, :] - c_im[None] * pw_i[:, :, None, :]
    ca_i = c_re[None] * pw_i[:, :, None, :] + c_im[None] * pw_r[:, :, None, :]
    kd = (jnp.einsum('dgop,gpi->dgoi', ca_r[:L], bbr, precision=HI)
          - jnp.einsum('dgop,gpi->dgoi', ca_i[:L], bbi, precision=HI))
    jj = np.arange(L)
    lag = jj[None, :] - jj[:, None]
    mt = jnp.where((lag >= 0)[None, :, :, None, None],
                   jnp.transpose(kd[np.clip(lag, 0, L - 1)], (2, 0, 1, 4, 3)), 0.0)
    mt = jnp.transpose(mt, (0, 1, 3, 2, 4)).reshape(g, L * SSM_GROUP, L * SSM_GROUP)
    mg_r = jnp.transpose(ca_r[1:], (1, 3, 0, 2)).reshape(g, p, L * SSM_GROUP)
    mg_i = -jnp.transpose(ca_i[1:], (1, 3, 0, 2)).reshape(g, p, L * SSM_GROUP)
    rev_r, rev_i = pw_r[:L][::-1], pw_i[:L][::-1]
    mf_r = rev_r[..., None] * bbr[None] - rev_i[..., None] * bbi[None]
    mf_i = rev_r[..., None] * bbi[None] + rev_i[..., None] * bbr[None]
    mf_r = jnp.transpose(mf_r, (1, 0, 3, 2)).reshape(g, L * SSM_GROUP, p)
    mf_i = jnp.transpose(mf_i, (1, 0, 3, 2)).reshape(g, L * SSM_GROUP, p)

    def pair_diag(m):
        gg, r, c = m.shape
        m = m.reshape(gg // 2, 2, r, c)
        zero = jnp.zeros_like(m[:, 0])
        top = jnp.concatenate([m[:, 0], zero], axis=2)
        bot = jnp.concatenate([zero, m[:, 1]], axis=2)
        return jnp.concatenate([top, bot], axis=1)

    al_r = pw_r[L].reshape(g // 2, 1, 2 * p)
    al_i = pw_i[L].reshape(g // 2, 1, 2 * p)
    dsk = jnp.broadcast_to(d_skip.reshape(g // 2, 2, 1, SSM_GROUP), (g // 2, 2, L, SSM_GROUP))
    dsk = dsk.reshape(g // 2, 1, 2 * L * SSM_GROUP)
    return dict(mt=pair_diag(mt).astype(BF16), mg_r=pair_diag(mg_r).astype(BF16),
                mg_i=pair_diag(mg_i).astype(BF16), mf_r=pair_diag(mf_r).astype(BF16),
                mf_i=pair_diag(mf_i).astype(BF16), al_r=al_r, al_i=al_i, dsk=dsk)


def _s5_kernel(u_ref, s0r_ref, s0i_ref, mt_ref, mgr_ref, mgi_ref, mfr_ref, mfi_ref,
               alr_ref, ali_ref, dsk_ref,
               y_ref, sr_ref, si_ref,
               inr_scr, ini_scr, str_scr, sti_scr, *, n_chunks, nb):
    u = u_ref[...]
    ub = u.astype(BF16)
    inr_scr[...] = _dot(ub, mfr_ref[...])
    ini_scr[...] = _dot(ub, mfi_ref[...])
    alr = jnp.broadcast_to(alr_ref[...], (nb, 2 * SSM_STATE))
    ali = jnp.broadcast_to(ali_ref[...], (nb, 2 * SSM_STATE))

    def step(c, carry):
        sr, si = carry
        rows = pl.ds(pl.multiple_of(c * nb, nb), nb)
        str_scr[rows, :] = sr
        sti_scr[rows, :] = si
        nr = alr * sr - ali * si + inr_scr[rows, :]
        ni = alr * si + ali * sr + ini_scr[rows, :]
        return nr, ni

    sr, si = lax.fori_loop(0, n_chunks, step, (s0r_ref[...], s0i_ref[...]))
    sr_ref[...] = sr
    si_ref[...] = si
    y = _dot(ub, mt_ref[...])
    y += _dot(str_scr[...].astype(BF16), mgr_ref[...])
    y += _dot(sti_scr[...].astype(BF16), mgi_ref[...])
    y_ref[...] = y + dsk_ref[...] * u


def _s5_scan(u, s0_re, s0_im, mats, chunk):
    b, t, _ = u.shape
    L = chunk
    nc = t // L
    gp = N_SSM_GROUPS // 2
    w = 2 * L * SSM_GROUP
    up = u.reshape(b, nc, L, gp, 2, SSM_GROUP).transpose(3, 1, 0, 4, 2, 5).reshape(gp, nc * b, w)
    s0r = s0_re.reshape(b, gp, 2 * SSM_STATE).transpose(1, 0, 2)
    s0i = s0_im.reshape(b, gp, 2 * SSM_STATE).transpose(1, 0, 2)
    rows = nc * b
    pair3 = lambda r, c: pl.BlockSpec((None, r, c), lambda g: (g, 0, 0))
    y, sr, si = pl.pallas_call(
        functools.partial(_s5_kernel, n_chunks=nc, nb=b),
        grid=(gp,),
        in_specs=[pair3(rows, w), pair3(b, 128), pair3(b, 128),
                  pair3(w, w), pair3(128, w), pair3(128, w), pair3(w, 128), pair3(w, 128),
                  pair3(1, 128), pair3(1, 128), pair3(1, w)],
        out_specs=[pair3(rows, w), pair3(b, 128), pair3(b, 128)],
        out_shape=[jax.ShapeDtypeStruct((gp, rows, w), F32),
                   jax.ShapeDtypeStruct((gp, b, 128), F32),
                   jax.ShapeDtypeStruct((gp, b, 128), F32)],
        scratch_shapes=[pltpu.VMEM((rows, 128), F32) for _ in range(4)],
        compiler_params=_cparams("arbitrary"),
        name="s5_scan",
    )(up, s0r, s0i, mats['mt'], mats['mg_r'], mats['mg_i'], mats['mf_r'], mats['mf_i'],
      mats['al_r'], mats['al_i'], mats['dsk'])
    y = y.reshape(gp, nc, b, 2, L, SSM_GROUP).transpose(2, 1, 4, 0, 3, 5).reshape(b, t, D_SSM)
    sr = sr.transpose(1, 0, 2).reshape(b, N_SSM_GROUPS, SSM_STATE)
    si = si.transpose(1, 0, 2).reshape(b, N_SSM_GROUPS, SSM_STATE)
    return y, sr, si


def _sort_key(x):
    x = jnp.where(x == 0.0, 0.0, x)
    i = pltpu.bitcast(x, jnp.int32)
    return jnp.where(i < 0, i ^ jnp.int32(0x7FFFFFFF), i)


def _lane_halves(x):
    lane = lax.broadcasted_iota(jnp.int32, x.shape, 1) % LANES
    zero = jnp.zeros_like(x)
    return jnp.where(lane < 64, x, zero), jnp.where(lane >= 64, x, zero)


def _kth_largest_key(count_ge, rows, n_sel):
    def bit_step(i, x):
        cand = x + lax.shift_left(jnp.int32(1), 31 - i)
        return jnp.where(count_ge(cand) >= n_sel, cand, x)
    return lax.fori_loop(0, 32, bit_step, jnp.full((rows, 1), INT_MIN, jnp.int32))


def _dsa_prompt_kernel(q_ref, qi_ref, wi_ref, kd_ref, vd_ref, kid_ref, tri_ref, o_ref,
                       key_scr, m_scr, l_scr, acc_scr, *, tq, kb, n_sel):
    qt = pl.program_id(1)
    nkb = (qt * tq + tq + kb - 1) // kb
    tpos = qt * tq + lax.broadcasted_iota(jnp.int32, (tq, kb), 0)
    lane_pos = lax.broadcasted_iota(jnp.int32, (tq, kb), 1)

    qi = qi_ref[...]
    qi_parts = []
    for blk in range(N_IDX_HEADS // 2):
        qi_parts.extend(_lane_halves(qi[:, blk * LANES:(blk + 1) * LANES]))
    wi = wi_ref[...]

    def score_block(j, _):
        cols = pl.ds(pl.multiple_of(j * kb, kb), kb)
        kid = kid_ref[cols, :]
        sc = jnp.zeros((tq, kb), F32)
        for h in range(N_IDX_HEADS):
            sc += jnp.maximum(_dot_nt(qi_parts[h], kid), 0.0) * wi[:, h:h + 1]
        sc = jnp.where(j * kb + lane_pos <= tpos, sc, -jnp.inf)
        key_scr[:, cols] = _sort_key(sc)
        return 0
    lax.fori_loop(0, nkb, score_block, 0)

    def count(pred):
        def body(j, acc):
            blk = key_scr[:, pl.ds(pl.multiple_of(j * kb, kb), kb)]
            hit = jnp.where(pred(blk), 1.0, 0.0)
            for c in range(kb // LANES):
                acc += hit[:, c * LANES:(c + 1) * LANES]
            return acc
        acc = lax.fori_loop(0, nkb, body, jnp.zeros((tq, LANES), F32))
        return jnp.sum(acc, axis=-1, keepdims=True)

    thr = _kth_largest_key(lambda cand: count(lambda blk: blk >= cand), tq, n_sel)
    room = n_sel - count(lambda blk: blk > thr)

    m_scr[...] = jnp.full(m_scr.shape, NEG, F32)
    l_scr[...] = jnp.zeros(l_scr.shape, F32)
    acc_scr[...] = jnp.zeros(acc_scr.shape, F32)
    q = q_ref[...]
    q_parts = []
    for n in range(N_KV_HEADS):
        q_parts.extend(_lane_halves(q[:, n * LANES:(n + 1) * LANES]))

    def attend_block(j, tie_seen):
        cols = pl.ds(pl.multiple_of(j * kb, kb), kb)
        key = key_scr[:, cols]
        tie = jnp.where(key == thr, 1.0, 0.0)
        rank = tie_seen + _dot(tie.astype(BF16), tri_ref[...])
        sel = (key > thr) | ((tie > 0.0) & (rank <= room))
        sel = sel & (j * kb + lane_pos <= tpos)
        bias = jnp.where(sel, 0.0, NEG)
        kd = kd_ref[cols, :]
        vd = vd_ref[cols, :]
        for h in range(N_HEADS):
            n = h // 2
            s = _dot_nt(q_parts[h], kd[:, n * LANES:(n + 1) * LANES]) + bias
            m_prev = m_scr[h]
            m_new = jnp.maximum(m_prev, jnp.max(s, axis=-1, keepdims=True))
            alpha = jnp.exp(m_prev - m_new)
            p = jnp.exp(s - m_new[:, :1])
            l_scr[h] = alpha * l_scr[h] + jnp.sum(p, axis=-1, keepdims=True)
            acc_scr[h] = alpha * acc_scr[h] + _dot(p.astype(BF16), vd[:, n * LANES:(n + 1) * LANES])
            m_scr[h] = m_new
        return tie_seen + jnp.sum(tie, axis=-1, keepdims=True)
    lax.fori_loop(0, nkb, attend_block, jnp.zeros((tq, 1), F32))

    lane = lax.broadcasted_iota(jnp.int32, (tq, LANES), 1)
    for n in range(N_KV_HEADS):
        lo = acc_scr[2 * n] / l_scr[2 * n]
        hi = acc_scr[2 * n + 1] / l_scr[2 * n + 1]
        o_ref[:, n * LANES:(n + 1) * LANES] = jnp.where(lane < 64, lo, hi)


def _upper_tri(n):
    r = np.arange(n)
    return jnp.asarray(r[:, None] <= r[None, :], BF16)


def _dsa_prompt(q, qi, wi, kd, vd, kid, b, s_len, tq=128, kb=512):
    kb = min(kb, s_len)
    n_sel = min(TOPK_KEYS, s_len // 4)
    nq = s_len // tq
    qspec = lambda w: pl.BlockSpec((tq, w), lambda bi, qi_: (bi * nq + qi_, 0))
    kspec = lambda w: pl.BlockSpec((s_len, w), lambda bi, qi_: (bi, 0))
    return pl.pallas_call(
        functools.partial(_dsa_prompt_kernel, tq=tq, kb=kb, n_sel=n_sel),
        grid=(b, nq),
        in_specs=[qspec(512), qspec(256), qspec(N_IDX_HEADS), kspec(512), kspec(512), kspec(128),
                  pl.BlockSpec((kb, kb), lambda bi, qi_: (0, 0))],
        out_specs=qspec(D_ATT),
        out_shape=jax.ShapeDtypeStruct((b * s_len, D_ATT), F32),
        scratch_shapes=[pltpu.VMEM((tq, s_len), jnp.int32),
                        pltpu.VMEM((N_HEADS, tq, LANES), F32),
                        pltpu.VMEM((N_HEADS, tq, LANES), F32),
                        pltpu.VMEM((N_HEADS, tq, LANES), F32)],
        compiler_params=_cparams("arbitrary", "arbitrary"),
        name="dsa_prompt",
    )(q, qi, wi, kd, vd, kid, _upper_tri(kb))


def _dsa_sample_kernel(pt_ref, qbd_ref, qis_ref, wi_ref, knew_ref, vnew_ref, kinew_ref, tri_ref,
                       ck_ref, cv_ref, cki_ref, o_ref,
                       kibuf, kbuf, vbuf, key_scr, m_scr, l_scr, acc_scr, sem_ki, sem_kv,
                       *, n_pages, n_sel, kb, ts):
    b = pl.program_id(0)
    nb = pl.num_programs(0)
    slot = b % 2
    past = n_pages * PAGE_SIZE
    rows = N_HEADS * ts

    def page_rows(p):
        return pl.ds(pl.multiple_of(p * PAGE_SIZE, PAGE_SIZE), PAGE_SIZE)

    def ki_copy(bb, sl, p):
        return pltpu.make_async_copy(cki_ref.at[pt_ref[bb, p]], kibuf.at[sl, page_rows(p), :], sem_ki.at[sl])

    def k_copy(bb, p):
        return pltpu.make_async_copy(ck_ref.at[pt_ref[bb, p]], kbuf.at[page_rows(p), :], sem_kv.at[0])

    def v_copy(bb, p):
        return pltpu.make_async_copy(cv_ref.at[pt_ref[bb, p]], vbuf.at[page_rows(p), :], sem_kv.at[1])

    def for_pages(fn):
        def body(p, _):
            fn(p)
            return 0
        lax.fori_loop(0, n_pages, body, 0)

    @pl.when(b == 0)
    def _():
        for_pages(lambda p: ki_copy(b, slot, p).start())

    def start_kv(p):
        k_copy(b, p).start()
        v_copy(b, p).start()
    for_pages(start_kv)

    @pl.when(b + 1 < nb)
    def _():
        for_pages(lambda p: ki_copy(b + 1, 1 - slot, p).start())

    for_pages(lambda p: ki_copy(b, slot, p).wait())

    qis = qis_ref[...]
    wi = wi_ref[...]

    def index_scores(kib):
        d = jnp.maximum(_dot_nt(qis, kib.astype(BF16)), 0.0)
        sc = jnp.zeros((ts, kib.shape[0]), F32)
        for h in range(N_IDX_HEADS):
            sc += d[h * ts:(h + 1) * ts] * wi[:, h:h + 1]
        return sc

    def score_block(j, _):
        cols = pl.ds(pl.multiple_of(j * kb, kb), kb)
        key_scr[:, cols] = _sort_key(index_scores(kibuf[slot, cols, :]))
        return 0
    lax.fori_loop(0, past // kb, score_block, 0)
    pad_rows = LANES - ts
    kin = jnp.concatenate([kinew_ref[...], jnp.zeros((pad_rows, IDX_DIM), F32)], axis=0)
    new_ok = (lax.broadcasted_iota(jnp.int32, (ts, LANES), 1)
              <= lax.broadcasted_iota(jnp.int32, (ts, LANES), 0))
    key_scr[:, past:past + LANES] = _sort_key(jnp.where(new_ok, index_scores(kin), -jnp.inf))

    def count(pred):
        return jnp.sum(jnp.where(pred(key_scr[...]), 1.0, 0.0), axis=-1, keepdims=True)

    thr = _kth_largest_key(lambda cand: count(lambda k_: k_ >= cand), ts, n_sel)
    room = n_sel - count(lambda k_: k_ > thr)

    for_pages(lambda p: k_copy(b, p).wait())
    for_pages(lambda p: v_copy(b, p).wait())

    m_scr[...] = jnp.full(m_scr.shape, NEG, F32)
    l_scr[...] = jnp.zeros(l_scr.shape, F32)
    acc_scr[...] = jnp.zeros(acc_scr.shape, F32)
    qbd = qbd_ref[...]

    def attend(key, kblk, vblk, ok, tie_seen, tri):
        tie = jnp.where(key == thr, 1.0, 0.0)
        rank = tie_seen + _dot(tie.astype(BF16), tri)
        sel = (key > thr) | ((tie > 0.0) & (rank <= room))
        if ok is not None:
            sel = sel & ok
        bias = jnp.where(sel, 0.0, NEG)
        bias = jnp.concatenate([bias] * N_HEADS, axis=0)
        s = _dot_nt(qbd, kblk.astype(BF16)) + bias
        m_prev = m_scr[...]
        m_new = jnp.maximum(m_prev, jnp.max(s, axis=-1, keepdims=True))
        alpha = jnp.exp(m_prev - m_new)
        p = jnp.exp(s - m_new[:, :1])
        l_scr[...] = alpha * l_scr[...] + jnp.sum(p, axis=-1, keepdims=True)
        acc_scr[...] = (jnp.concatenate([alpha, alpha], axis=1) * acc_scr[...]
                        + _dot(p.astype(BF16), vblk.astype(BF16)))
        m_scr[...] = m_new
        return tie_seen + jnp.sum(tie, axis=-1, keepdims=True)

    def attend_block(j, tie_seen):
        cols = pl.ds(pl.multiple_of(j * kb, kb), kb)
        return attend(key_scr[:, cols], kbuf[cols, :], vbuf[cols, :], None, tie_seen, tri_ref[...])
    tie_seen = lax.fori_loop(0, past // kb, attend_block, jnp.zeros((ts, 1), F32))
    zpad = jnp.zeros((pad_rows, N_KV_HEADS * HEAD_DIM), F32)
    attend(key_scr[:, past:past + LANES],
           jnp.concatenate([knew_ref[...], zpad], axis=0),
           jnp.concatenate([vnew_ref[...], zpad], axis=0),
           new_ok, tie_seen, tri_ref[:LANES, :LANES])
    l = l_scr[...]
    o_ref[...] = acc_scr[...] / jnp.concatenate([l, l], axis=1)


def _dsa_sample(q, qi, wi, k, v, ki, cache_k, cache_v, cache_kidx, page_table, ts, kb=512):
    bd, n_pages = page_table.shape
    n_pool = cache_k.shape[0]
    past = n_pages * PAGE_SIZE
    n_sel = min(TOPK_KEYS, (past + ts) // 4)
    rows = N_HEADS * ts
    kvw = N_KV_HEADS * HEAD_DIM
    qh = q.reshape(bd, ts, N_KV_HEADS, 2, HEAD_DIM).transpose(0, 2, 3, 1, 4)
    eye = jnp.eye(N_KV_HEADS, dtype=q.dtype)
    qbd = (qh[:, :, :, :, None, :] * eye[None, :, None, None, :, None]).reshape(bd, rows, kvw)
    qis = qi.reshape(bd, ts, N_IDX_HEADS, IDX_DIM).transpose(0, 2, 1, 3).reshape(bd, N_IDX_HEADS * ts, IDX_DIM)
    per_b = lambda r, c: pl.BlockSpec((None, r, c), lambda i, pt: (i, 0, 0))
    any_spec = pl.BlockSpec(memory_space=pl.ANY)
    o = pl.pallas_call(
        functools.partial(_dsa_sample_kernel, n_pages=n_pages, n_sel=n_sel, kb=kb, ts=ts),
        grid_spec=pltpu.PrefetchScalarGridSpec(
            num_scalar_prefetch=1,
            grid=(bd,),
            in_specs=[per_b(rows, kvw), per_b(N_IDX_HEADS * ts, IDX_DIM), per_b(ts, N_IDX_HEADS),
                      per_b(ts, kvw), per_b(ts, kvw), per_b(ts, IDX_DIM),
                      pl.BlockSpec((kb, kb), lambda i, pt: (0, 0)),
                      any_spec, any_spec, any_spec],
            out_specs=per_b(rows, kvw),
            scratch_shapes=[pltpu.VMEM((2, past, IDX_DIM), F32),
                            pltpu.VMEM((past, kvw), F32),
                            pltpu.VMEM((past, kvw), F32),
                            pltpu.VMEM((ts, past + LANES), jnp.int32),
                            pltpu.VMEM((rows, LANES), F32),
                            pltpu.VMEM((rows, LANES), F32),
                            pltpu.VMEM((rows, kvw), F32),
                            pltpu.SemaphoreType.DMA((2,)),
                            pltpu.SemaphoreType.DMA((2,))]),
        out_shape=jax.ShapeDtypeStruct((bd, rows, kvw), F32),
        compiler_params=_cparams("arbitrary"),
        name="dsa_sample",
    )(page_table, qbd, qis, wi.reshape(bd, ts, N_IDX_HEADS), k.reshape(bd, ts, kvw),
      v.reshape(bd, ts, kvw), ki.reshape(bd, ts, IDX_DIM), _upper_tri(kb),
      cache_k.reshape(n_pool, PAGE_SIZE, kvw), cache_v.reshape(n_pool, PAGE_SIZE, kvw), cache_kidx)
    o = o.reshape(bd, N_KV_HEADS, 2, ts, N_KV_HEADS, HEAD_DIM)
    o = jnp.einsum('bngtnd->btngd', o)
    return o.reshape(bd * ts, D_ATT)


def _layer_norm(x, g, b):
    mu = jnp.mean(x, axis=-1, keepdims=True)
    xc = x - mu
    var = jnp.mean(xc * xc, axis=-1, keepdims=True)
    return xc * lax.rsqrt(var + LN_EPS) * g + b


def _rms_norm(x, g):
    return x * lax.rsqrt(jnp.mean(x * x, axis=-1, keepdims=True) + LN_EPS) * g


def _mix_kernel(ys_ref, ya_ref, x_ref, g1_ref, sc2_ref, sh2_ref, wglu_ref, bglu_ref, gs_ref, ga_ref,
                wout_ref, l1g_ref, l1b_ref, wr_ref, br_ref,
                x1_ref, h2_ref, eidx_ref, gate_ref):
    g = jax.nn.gelu(ys_ref[...])
    ssm = g * jax.nn.sigmoid(_dot(g.astype(BF16), wglu_ref[...]) + bglu_ref[...])
    cat = jnp.concatenate([_rms_norm(ssm, gs_ref[...]), _rms_norm(ya_ref[...], ga_ref[...])], axis=-1)
    mixed = _dot(cat.astype(BF16), wout_ref[...])
    x1 = _layer_norm(ALPHA * x_ref[...] + g1_ref[...] * mixed, l1g_ref[...], l1b_ref[...])
    x1_ref[...] = x1
    h2 = (x1 * (1.0 + sc2_ref[...]) + sh2_ref[...]).astype(BF16)
    h2_ref[...] = h2
    aff = jax.nn.sigmoid(_dot(h2, wr_ref[...]))
    vals = aff + br_ref[...]
    tm = aff.shape[0]
    lane = lax.broadcasted_iota(jnp.int32, (tm, N_EXPERTS), 1)
    col = lax.broadcasted_iota(jnp.int32, (tm, 8), 1)
    eidx = jnp.zeros((tm, 8), jnp.int32)
    gsel = jnp.zeros((tm, 8), F32)
    for j in range(TOP_K_EXPERTS):
        top = jnp.max(vals, axis=-1, keepdims=True)
        idx = jnp.min(jnp.where(vals == top, lane, N_EXPERTS), axis=-1, keepdims=True)
        hit = lane == idx
        gj = jnp.sum(jnp.where(hit, aff, 0.0), axis=-1, keepdims=True)
        eidx = jnp.where(col == j, idx, eidx)
        gsel = jnp.where(col == j, gj, gsel)
        vals = jnp.where(hit, -jnp.inf, vals)
    eidx_ref[...] = eidx
    gate_ref[...] = gsel / jnp.sum(gsel, axis=-1, keepdims=True) * ROUTED_SCALE


def _mixer_out(ys, ya, x, g1, sc2, sh2, w, rows_per_mod, tm):
    t = x.shape[0]
    ms = _mod_spec(g1, tm, rows_per_mod)
    row = lambda wd: pl.BlockSpec((tm, wd), lambda i: (i, 0))
    full = lambda a: pl.BlockSpec(a.shape, lambda i: (0, 0))
    consts = [w['w_glu'], w['b_glu'], w['g_ssm_out'], w['g_att_out'], w['w_out'], w['ln1_g'], w['ln1_b'],
              w['w_router'], w['b_router']]
    return pl.pallas_call(
        _mix_kernel,
        grid=(t // tm,),
        in_specs=[row(D_SSM), row(D_ATT), row(D_MODEL), ms, ms, ms] + [full(a) for a in consts],
        out_specs=[row(D_MODEL), row(D_MODEL), row(8), row(8)],
        out_shape=[jax.ShapeDtypeStruct((t, D_MODEL), F32), jax.ShapeDtypeStruct((t, D_MODEL), BF16),
                   jax.ShapeDtypeStruct((t, 8), jnp.int32), jax.ShapeDtypeStruct((t, 8), F32)],
        compiler_params=_cparams("arbitrary"),
        name="mixer_out",
    )(ys, ya, x, g1, sc2, sh2, *consts)


MOE_ROWS = 256


def _experts_kernel(be_ref, nused_ref, x_ref, gate_ref, w1_ref, w3_ref, w2_ref, y_ref):
    i = pl.program_id(0)

    @pl.when(i < nused_ref[0])
    def _():
        x = x_ref[...]
        h1 = _dot(x, w1_ref[...].astype(BF16))
        h3 = _dot(x, w3_ref[...].astype(BF16))
        a = (h1 * jax.nn.sigmoid(h1) * h3).astype(BF16)
        y_ref[...] = _dot(a, w2_ref[...].astype(BF16)) * gate_ref[...]

    @pl.when(i >= nused_ref[0])
    def _():
        y_ref[...] = jnp.zeros(y_ref.shape, F32)


def _experts(xs, slot_gate, block_e, n_used, w_e1, w_e3, w_e2):
    n_slots = xs.shape[0]
    n_blocks = n_slots // MOE_ROWS
    wspec = lambda r, c: pl.BlockSpec((None, r, c), lambda i, be, nu: (be[i], 0, 0))
    return pl.pallas_call(
        _experts_kernel,
        grid_spec=pltpu.PrefetchScalarGridSpec(
            num_scalar_prefetch=2,
            grid=(n_blocks,),
            in_specs=[pl.BlockSpec((MOE_ROWS, D_MODEL), lambda i, be, nu: (i, 0)),
                      pl.BlockSpec((MOE_ROWS, 1), lambda i, be, nu: (i, 0)),
                      wspec(D_MODEL, D_EXPERT), wspec(D_MODEL, D_EXPERT), wspec(D_EXPERT, D_MODEL)],
            out_specs=pl.BlockSpec((MOE_ROWS, D_MODEL), lambda i, be, nu: (i, 0))),
        out_shape=jax.ShapeDtypeStruct((n_slots, D_MODEL), F32),
        compiler_params=_cparams("arbitrary"),
        name="routed_experts",
    )(block_e, n_used, xs, slot_gate, w_e1, w_e3, w_e2)


def _dispatch_plan(eidx, n_tok):
    onehot = (eidx[:, :, None] == jnp.arange(N_EXPERTS, dtype=jnp.int32)[None, None, :]).sum(1).astype(jnp.int32)
    before = jnp.cumsum(onehot, axis=0) - onehot
    counts = jnp.sum(onehot, axis=0)
    padded = (counts + MOE_ROWS - 1) // MOE_ROWS * MOE_ROWS
    pends = jnp.cumsum(padded)
    pstarts = pends - padded
    slot = pstarts[eidx] + jnp.take_along_axis(before, eidx, axis=1)
    n_blocks = -(-n_tok * TOP_K_EXPERTS // MOE_ROWS) + N_EXPERTS
    block_e = jnp.minimum(jnp.searchsorted(pends, jnp.arange(n_blocks) * MOE_ROWS, side='right'),
                          N_EXPERTS - 1).astype(jnp.int32)
    n_used = (pends[-1:] // MOE_ROWS).astype(jnp.int32)
    return slot, block_e, n_used, n_blocks * MOE_ROWS


def _final_kernel(r_ref, h2_ref, x1_ref, g2_ref, ws1_ref, ws3_ref, ws2_ref, l2g_ref, l2b_ref, o_ref):
    h2 = h2_ref[...]
    h1 = _dot(h2, ws1_ref[...])
    h3 = _dot(h2, ws3_ref[...])
    a = (h1 * jax.nn.sigmoid(h1) * h3).astype(BF16)
    ffn = r_ref[...] + _dot(a, ws2_ref[...])
    o_ref[...] = _layer_norm(ALPHA * x1_ref[...] + g2_ref[...] * ffn, l2g_ref[...], l2b_ref[...])


def _final(routed, h2, x1, g2, w, rows_per_mod, tm):
    t = x1.shape[0]
    row = lambda wd: pl.BlockSpec((tm, wd), lambda i: (i, 0))
    full = lambda a: pl.BlockSpec(a.shape, lambda i: (0, 0))
    consts = [w['w_s1'], w['w_s3'], w['w_s2'], w['ln2_g'], w['ln2_b']]
    return pl.pallas_call(
        _final_kernel,
        grid=(t // tm,),
        in_specs=[row(D_MODEL), row(D_MODEL), row(D_MODEL), _mod_spec(g2, tm, rows_per_mod)]
                 + [full(a) for a in consts],
        out_specs=row(D_MODEL),
        out_shape=jax.ShapeDtypeStruct((t, D_MODEL), F32),
        compiler_params=_cparams("arbitrary"),
        name="final_residual",
    )(routed, h2, x1, g2, *consts)


def kernel(x_prompt, x_sample, c_prompt, c_sample, cache_k, cache_v, cache_kidx, state_ssm_re, state_ssm_im, page_table, w_ada, b_ada, w_in, ssm_a_re, ssm_a_im, ssm_log_dt, ssm_b_re, ssm_b_im, ssm_c_re, ssm_c_im, ssm_d, w_glu, b_glu, g_ssm_out, g_att_out, w_out, ln1_g, ln1_b, w_router, b_router, w_e1, w_e3, w_e2, w_s1, w_s3, w_s2, ln2_g, ln2_b):
    assert w_in.shape[0] == 1, "one layer"
    b, s, d = x_prompt.shape
    bd, ts, _ = x_sample.shape
    tp, tsn = b * s, bd * ts
    tm = 512
    row = lambda a: a[0].reshape(1, -1)
    w = dict(w_glu=w_glu[0].astype(BF16), b_glu=row(b_glu), g_ssm_out=row(g_ssm_out), g_att_out=row(g_att_out),
             w_out=w_out[0].astype(BF16), ln1_g=row(ln1_g), ln1_b=row(ln1_b),
             w_router=w_router[0].astype(BF16), b_router=row(b_router),
             w_s1=w_s1[0].astype(BF16), w_s3=w_s3[0].astype(BF16), w_s2=w_s2[0].astype(BF16),
             ln2_g=row(ln2_g), ln2_b=row(ln2_b))
    w_wide = _widen_w_in(w_in[0])
    ssm_args = (ssm_a_re[0], ssm_a_im[0], ssm_log_dt[0], ssm_b_re[0], ssm_b_im[0], ssm_c_re[0], ssm_c_im[0],
                ssm_d[0])

    mod = _ada_mod(jnp.concatenate([c_prompt, c_sample], axis=0), w_ada[0], b_ada[0])
    mod_p = mod[:b].reshape(b, 6, 1, d)
    mod_s = jnp.broadcast_to(mod[b:].reshape(bd, 1, 6, d), (bd, ts, 6, d)).reshape(tsn, 6, d)

    xp = x_prompt.reshape(tp, d)
    u, q, k_p, v_p, qi, ki_p, wi, kd, vd, kid = _in_projection(xp, mod_p[:, 1], mod_p[:, 0], w_wide, s, tm)
    chunk_p = 16
    zero_state = jnp.zeros((b, N_SSM_GROUPS, SSM_STATE), F32)
    ys_p, re_p, im_p = _s5_scan(u.reshape(b, s, D_SSM), zero_state, zero_state,
                                _ssm_matrices(*ssm_args, chunk_p), chunk_p)
    ya_p = _dsa_prompt(q, qi, wi, kd, vd, kid, b, s)
    x1_p, h2_p, eidx_p, gate_p = _mixer_out(ys_p.reshape(tp, D_SSM), ya_p, xp, mod_p[:, 2], mod_p[:, 4],
                                            mod_p[:, 3], w, s, tm)

    xs = x_sample.reshape(tsn, d)
    tms = min(tm, tsn)
    u, q, k_s, v_s, qi, ki_s, wi, _, _, _ = _in_projection(xs, mod_s[:, 1], mod_s[:, 0], w_wide, None, tms)
    ys_s, re_s, im_s = _s5_scan(u.reshape(bd, ts, D_SSM), state_ssm_re[0], state_ssm_im[0],
                                _ssm_matrices(*ssm_args, ts), ts)
    ya_s = _dsa_sample(q, qi, wi, k_s, v_s, ki_s, cache_k[0], cache_v[0], cache_kidx[0], page_table, ts)
    x1_s, h2_s, eidx_s, gate_s = _mixer_out(ys_s.reshape(tsn, D_SSM), ya_s, xs, mod_s[:, 2], mod_s[:, 4],
                                            mod_s[:, 3], w, None, tms)

    n_tok = tp + tsn
    h2 = jnp.concatenate([h2_p, h2_s], axis=0)
    eidx = jnp.concatenate([eidx_p, eidx_s], axis=0)[:, :TOP_K_EXPERTS]
    gates = jnp.concatenate([gate_p, gate_s], axis=0)[:, :TOP_K_EXPERTS]
    slot, block_e, n_used, n_slots = _dispatch_plan(eidx, n_tok)
    tok = jnp.broadcast_to(jnp.arange(n_tok, dtype=jnp.int32)[:, None], slot.shape)
    slot_tok = jnp.zeros((n_slots,), jnp.int32).at[slot.reshape(-1)].set(tok.reshape(-1))
    slot_gate = jnp.zeros((n_slots,), F32).at[slot.reshape(-1)].set(gates.reshape(-1))
    ye = _experts(h2[slot_tok], slot_gate.reshape(n_slots, 1), block_e, n_used, w_e1[0], w_e3[0], w_e2[0])
    routed = jnp.sum(ye[slot], axis=1)

    y_p = _final(routed[:tp], h2_p, x1_p, mod_p[:, 5], w, s, tm)
    y_s = _final(routed[tp:], h2_s, x1_s, mod_s[:, 5], w, None, tms)

    kvs = (1, b, s, N_KV_HEADS, HEAD_DIM)
    kvd = (1, bd, ts, N_KV_HEADS, HEAD_DIM)
    return (y_p.reshape(b, s, d), y_s.reshape(bd, ts, d),
            k_p.reshape(kvs), v_p.reshape(kvs), ki_p.reshape(1, b, s, IDX_DIM), re_p[None], im_p[None],
            k_s.reshape(kvd), v_s.reshape(kvd), ki_s.reshape(1, bd, ts, IDX_DIM), re_s[None], im_s[None])
```

```python
import functools
import math

import jax
import jax.numpy as jnp
import numpy as np
from jax import lax
from jax.experimental import pallas as pl
from jax.experimental.pallas import tpu as pltpu

D_MODEL = 1024
D_SSM = 512
SSM_GROUP = 16
N_SSM_GROUPS = 32
SSM_STATE = 64
D_ATT = 512
HEAD_DIM = 64
N_HEADS = 8
N_KV_HEADS = 4
N_IDX_HEADS = 4
IDX_DIM = 64
TOPK_KEYS = 256
N_EXPERTS = 64
TOP_K_EXPERTS = 6
D_EXPERT = 256
ROUTED_SCALE = 2.5
PAGE_SIZE = 128
ALPHA = 2.0 ** 0.25
LN_EPS = 1e-5

LANES = 128
VMEM_LIMIT = 56 * 1024 * 1024
NEG = -1e30
INT_MIN = -2 ** 31

F32 = jnp.float32
BF16 = jnp.bfloat16
HI = lax.Precision.HIGHEST


def _cparams(*sem):
    return pltpu.CompilerParams(dimension_semantics=sem, vmem_limit_bytes=VMEM_LIMIT)


def _dot(a, b):
    return jnp.dot(a, b, preferred_element_type=F32)


def _dot_nt(a, b):
    return lax.dot_general(a, b, (((1,), (1,)), ((), ())), preferred_element_type=F32)


def _ada_kernel(c_ref, w_ref, b_ref, o_ref):
    c = c_ref[...]
    s = (c * jax.nn.sigmoid(c)).astype(BF16)
    o_ref[...] = _dot(s, w_ref[...].astype(BF16)) + b_ref[...]


def _ada_mod(c, w_ada, b_ada):
    n, d = c.shape
    dn = w_ada.shape[1]
    tn = 1024
    return pl.pallas_call(
        _ada_kernel,
        grid=(dn // tn,),
        in_specs=[pl.BlockSpec((n, d), lambda j: (0, 0)),
                  pl.BlockSpec((d, tn), lambda j: (0, j)),
                  pl.BlockSpec((1, tn), lambda j: (0, j))],
        out_specs=pl.BlockSpec((n, tn), lambda j: (0, j)),
        out_shape=jax.ShapeDtypeStruct((n, dn), F32),
        compiler_params=_cparams("arbitrary"),
        name="ada_mod",
    )(c, w_ada, b_ada.reshape(1, dn))


_C_U, _C_Q, _C_K, _C_V, _C_QI, _C_KI, _C_WI, _C_END = 0, 512, 1024, 1280, 1536, 1792, 1920, 2048
_P_U, _P_K, _P_V, _P_KI, _P_END = 0, 512, 768, 1024, 1152
_T_Q, _T_QI, _T_V, _T_WI, _T_END = 0, 512, 768, 1024, 1032


def _projection_weights(w_in):
    d = w_in.shape[0]
    cuts = np.cumsum([D_SSM, N_HEADS * HEAD_DIM, N_KV_HEADS * HEAD_DIM, N_KV_HEADS * HEAD_DIM,
                      N_IDX_HEADS * IDX_DIM, IDX_DIM])
    wu, wq, wk, wv, wqi, wki, wwi = jnp.split(w_in, cuts.tolist(), axis=1)
    wq, wqi, wwi = wq * HEAD_DIM ** -0.5, wqi * IDX_DIM ** -0.5, wwi * N_IDX_HEADS ** -0.5
    z = lambda n: jnp.zeros((d, n), w_in.dtype)
    sample = jnp.concatenate([wu, wq, wk, wv, wqi, wki, z(64), wwi, z(124)], axis=1)
    prompt = jnp.concatenate([wu, wk, wv, wki, z(64)], axis=1)
    prompt_t = jnp.concatenate([wq, wqi, wv, wwi, z(4)], axis=1).T
    return sample.astype(BF16), prompt.astype(BF16), prompt_t.astype(BF16)


def _inproj_kernel(x_ref, sc_ref, sh_ref, w_ref, u_ref, q_ref, k_ref, v_ref, qi_ref, ki_ref, wi_ref):
    h = (x_ref[...] * (1.0 + sc_ref[...]) + sh_ref[...]).astype(BF16)
    r = _dot(h, w_ref[...])
    u_ref[...] = r[:, _C_U:_C_Q]
    q_ref[...] = r[:, _C_Q:_C_K].astype(BF16)
    k_ref[...] = r[:, _C_K:_C_V]
    v_ref[...] = r[:, _C_V:_C_QI]
    qi_ref[...] = r[:, _C_QI:_C_KI].astype(BF16)
    ki_ref[...] = r[:, _C_KI:_C_KI + IDX_DIM]
    wi_ref[...] = r[:, _C_WI:_C_WI + N_IDX_HEADS]


def _inproj_prompt_kernel(x_ref, sc_ref, sh_ref, w_ref, wt_ref,
                          u_ref, k_ref, v_ref, ki_ref, kb_ref, kib_ref, qt_ref, qit_ref, vt_ref, wit_ref):
    h = (x_ref[...] * (1.0 + sc_ref[...]) + sh_ref[...]).astype(BF16)
    r = _dot(h, w_ref[...])
    u_ref[...] = r[:, _P_U:_P_K]
    k = r[:, _P_K:_P_V]
    k_ref[...] = k
    kb_ref[...] = k.astype(BF16)
    v_ref[...] = r[:, _P_V:_P_KI]
    ki = r[:, _P_KI:_P_KI + IDX_DIM]
    ki_ref[...] = ki
    kib_ref[...] = ki.astype(BF16)
    rt = _dot_nt(wt_ref[...], h)
    qt_ref[...] = rt[_T_Q:_T_QI].astype(BF16)
    qit_ref[...] = rt[_T_QI:_T_V].astype(BF16)
    vt_ref[...] = rt[_T_V:_T_WI].astype(BF16)
    wit_ref[...] = rt[_T_WI:_T_END]


def _mod_spec(mod, tm, rows_per_mod):
    if rows_per_mod is None:
        return pl.BlockSpec((tm, D_MODEL), lambda i: (i, 0))
    tiles = rows_per_mod // tm
    return pl.BlockSpec((None, 1, D_MODEL), lambda i: (i // tiles, 0, 0))


def _in_projection(x, sc, sh, w_tok, rows_per_mod, tm):
    t = x.shape[0]
    widths = [(D_SSM, F32), (512, BF16), (256, F32), (256, F32), (256, BF16), (IDX_DIM, F32),
              (N_IDX_HEADS, F32)]
    ms = _mod_spec(sc, tm, rows_per_mod)
    return pl.pallas_call(
        _inproj_kernel,
        grid=(t // tm,),
        in_specs=[pl.BlockSpec((tm, D_MODEL), lambda i: (i, 0)), ms, ms,
                  pl.BlockSpec(w_tok.shape, lambda i: (0, 0))],
        out_specs=[pl.BlockSpec((tm, w), lambda i: (i, 0)) for w, _ in widths],
        out_shape=[jax.ShapeDtypeStruct((t, w), dt) for w, dt in widths],
        compiler_params=_cparams("arbitrary"),
        name="in_projection",
    )(x, sc, sh, w_tok)


def _in_projection_prompt(x, sc, sh, w_tok, w_feat, rows_per_mod, tm):
    t = x.shape[0]
    tok = [(D_SSM, F32), (256, F32), (256, F32), (IDX_DIM, F32), (256, BF16), (IDX_DIM, BF16)]
    feat = [(512, BF16), (256, BF16), (256, BF16), (8, F32)]
    ms = _mod_spec(sc, tm, rows_per_mod)
    return pl.pallas_call(
        _inproj_prompt_kernel,
        grid=(t // tm,),
        in_specs=[pl.BlockSpec((tm, D_MODEL), lambda i: (i, 0)), ms, ms,
                  pl.BlockSpec(w_tok.shape, lambda i: (0, 0)), pl.BlockSpec(w_feat.shape, lambda i: (0, 0))],
        out_specs=[pl.BlockSpec((tm, w), lambda i: (i, 0)) for w, _ in tok]
                  + [pl.BlockSpec((r, tm), lambda i: (0, i)) for r, _ in feat],
        out_shape=[jax.ShapeDtypeStruct((t, w), dt) for w, dt in tok]
                  + [jax.ShapeDtypeStruct((r, t), dt) for r, dt in feat],
        compiler_params=_cparams("arbitrary"),
        name="in_projection_prompt",
    )(x, sc, sh, w_tok, w_feat)


S5_CHUNK = 8
S5_BLOCKS = D_SSM // LANES
S5_GROUPS_PER_BLOCK = LANES // SSM_GROUP
S5_PAIRS = N_SSM_GROUPS // 2


def _ssm_matrices(a_re, a_im, log_dt, b_re, b_im, c_re, c_im, d_skip):
    g, p = a_re.shape
    L = S5_CHUNK
    dt = jnp.exp(log_dt)[:, None]
    mag = jnp.exp(dt * a_re)
    abr, abi = mag * jnp.cos(dt * a_im), mag * jnp.sin(dt * a_im)
    den = a_re * a_re + a_im * a_im
    nr, ni = abr - 1.0, abi
    fr = (nr * a_re + ni * a_im) / den
    fi = (ni * a_re - nr * a_im) / den
    bbr = fr[..., None] * b_re - fi[..., None] * b_im
    bbi = fr[..., None] * b_im + fi[..., None] * b_re
    dd = jnp.arange(L + 1, dtype=F32)[:, None, None]
    pm = jnp.exp(dd * (dt * a_re)[None])
    pw_r = pm * jnp.cos(dd * (dt * a_im)[None])
    pw_i = pm * jnp.sin(dd * (dt * a_im)[None])
    ca_r = c_re[None] * pw_r[:, :, None, :] - c_im[None] * pw_i[:, :, None, :]
    ca_i = c_re[None] * pw_i[:, :, None, :] + c_im[None] * pw_r[:, :, None, :]
    kd = (jnp.einsum('dgop,gpi->dgoi', ca_r[:L], bbr, precision=HI)
          - jnp.einsum('dgop,gpi->dgoi', ca_i[:L], bbi, precision=HI))
    nblk, gblk, npair = S5_BLOCKS, S5_GROUPS_PER_BLOCK, S5_PAIRS
    ppb = npair // nblk
    kd_t = jnp.transpose(kd, (1, 0, 3, 2)).reshape(nblk, gblk, L, SSM_GROUP, SSM_GROUP)
    mtd = jnp.einsum('qgdio,gh->qdgiho', kd_t, jnp.eye(gblk, dtype=F32))
    mtd = mtd.reshape(nblk, L, LANES, LANES)
    sel = np.zeros((ppb, gblk, 2), np.float32)
    for r in range(ppb):
        for t in range(2):
            sel[r, 2 * r + t, t] = 1.0
    sel = jnp.asarray(sel)
    rev_r, rev_i = pw_r[:L][::-1], pw_i[:L][::-1]
    f_r = rev_r[..., None] * bbr[None] - rev_i[..., None] * bbi[None]
    f_i = rev_r[..., None] * bbi[None] + rev_i[..., None] * bbr[None]

    def place_in(f):
        f = f.reshape(L, nblk, ppb, 2, p, SSM_GROUP)
        f = jnp.einsum('mqrtPi,rgt->qrmgitP', f, sel)
        return f.reshape(npair, L, LANES, 2 * p).astype(BF16)

    def place_out(c):
        c = c.reshape(L, nblk, ppb, 2, SSM_GROUP, p)
        c = jnp.einsum('jqrtoP,rgt->qrjtPgo', c, sel)
        return c.reshape(npair, L, 2 * p, LANES).astype(BF16)

    return dict(mtd=mtd.astype(BF16), mf_r=place_in(f_r), mf_i=place_in(f_i),
                mg_r=place_out(ca_r[1:]), mg_i=place_out(-ca_i[1:]),
                al_r=pw_r[L].reshape(npair, 1, 2 * p), al_i=pw_i[L].reshape(npair, 1, 2 * p),
                dsk=d_skip.reshape(nblk, 1, LANES))


def _s5_kernel(*refs, nseq, nc, chained):
    u_refs, refs = refs[:S5_BLOCKS], refs[S5_BLOCKS:]
    (s0r_ref, s0i_ref, mtd_ref, mfr_ref, mfi_ref, mgr_ref, mgi_ref, alr_ref, ali_ref, dsk_ref), refs = \
        refs[:10], refs[10:]
    y_refs, refs = refs[:S5_BLOCKS], refs[S5_BLOCKS:]
    sr_ref, si_ref, xb_scr, inr_scr, ini_scr, str_scr, sti_scr, sbr_scr, sbi_scr, car_r, car_i = refs
    L = S5_CHUNK
    rows = nseq * nc
    ppb = S5_PAIRS // S5_BLOCKS

    def x_block(m, q):
        return u_refs[q][:, pl.ds(m, nc, stride=L), :].reshape(rows, LANES)

    for m in range(L):
        for q in range(S5_BLOCKS):
            xb_scr[m, q] = x_block(m, q).astype(BF16)

    for p in range(S5_PAIRS):
        q = p // ppb
        acc_r = _dot(xb_scr[0, q], mfr_ref[p, 0])
        acc_i = _dot(xb_scr[0, q], mfi_ref[p, 0])
        for m in range(1, L):
            acc_r += _dot(xb_scr[m, q], mfr_ref[p, m])
            acc_i += _dot(xb_scr[m, q], mfi_ref[p, m])
        inr_scr[p] = acc_r
        ini_scr[p] = acc_i

    if chained:
        @pl.when(pl.program_id(0) == 0)
        def _():
            car_r[...] = s0r_ref[...]
            car_i[...] = s0i_ref[...]

        def step(c, carry):
            idx = pl.ds(c, nseq, stride=nc)
            out = []
            for p in range(S5_PAIRS):
                sr, si = carry[2 * p], carry[2 * p + 1]
                str_scr[p, idx, :] = sr
                sti_scr[p, idx, :] = si
                alr, ali = alr_ref[p], ali_ref[p]
                out.append(alr * sr - ali * si + inr_scr[p, idx, :])
                out.append(alr * si + ali * sr + ini_scr[p, idx, :])
            return tuple(out)

        init = []
        for p in range(S5_PAIRS):
            init += [car_r[p], car_i[p]]
        end = lax.fori_loop(0, nc, step, tuple(init))
        for p in range(S5_PAIRS):
            car_r[p] = end[2 * p]
            car_i[p] = end[2 * p + 1]
        sr_ref[...] = car_r[...]
        si_ref[...] = car_i[...]
    else:
        for p in range(S5_PAIRS):
            sr, si = s0r_ref[p], s0i_ref[p]
            str_scr[p] = sr
            sti_scr[p] = si
            alr, ali = alr_ref[p], ali_ref[p]
            sr_ref[p] = alr * sr - ali * si + inr_scr[p]
            si_ref[p] = alr * si + ali * sr + ini_scr[p]

    for p in range(S5_PAIRS):
        sbr_scr[p] = str_scr[p].astype(BF16)
        sbi_scr[p] = sti_scr[p].astype(BF16)

    for q in range(S5_BLOCKS):
        for j in range(L):
            acc = _dot(xb_scr[j, q], mtd_ref[q, 0])
            for m in range(j):
                acc += _dot(xb_scr[m, q], mtd_ref[q, j - m])
            for r in range(ppb):
                p = q * ppb + r
                acc += _dot(sbr_scr[p], mgr_ref[p, j])
                acc += _dot(sbi_scr[p], mgi_ref[p, j])
            y = acc + dsk_ref[q] * x_block(j, q)
            y_refs[q][:, pl.ds(j, nc, stride=L), :] = y.reshape(nseq, nc, LANES)


def _s5_scan(u, s0_re, s0_im, mats, chained, tt):
    nseq, t, _ = u.shape
    n_states = s0_re.shape[0]
    L = S5_CHUNK
    nc = tt // L
    rows = nseq * nc
    assert rows == n_states or chained
    pairs = lambda s: s.reshape(n_states, S5_PAIRS, 2 * SSM_STATE).transpose(1, 0, 2)
    full = lambda a: pl.BlockSpec(a.shape, lambda i: (0,) * a.ndim)
    consts = [mats[k] for k in ('mtd', 'mf_r', 'mf_i', 'mg_r', 'mg_i', 'al_r', 'al_i', 'dsk')]
    state_shape = jax.ShapeDtypeStruct((S5_PAIRS, n_states, 2 * SSM_STATE), F32)
    state_spec = pl.BlockSpec((S5_PAIRS, n_states, 2 * SSM_STATE), lambda i: (0, 0, 0))
    pair_rows = lambda dt: pltpu.VMEM((S5_PAIRS, rows, LANES), dt)
    lane_block = lambda q: pl.BlockSpec((nseq, tt, LANES), lambda i: (0, i, q))
    out_block = pl.BlockSpec((nseq, tt, LANES), lambda i: (0, i, 0))
    *y, sr, si = pl.pallas_call(
        functools.partial(_s5_kernel, nseq=nseq, nc=nc, chained=chained),
        grid=(t // tt,),
        in_specs=[lane_block(q) for q in range(S5_BLOCKS)] + [state_spec, state_spec]
                 + [full(a) for a in consts],
        out_specs=[out_block] * S5_BLOCKS + [state_spec, state_spec],
        out_shape=[jax.ShapeDtypeStruct((nseq, t, LANES), F32)] * S5_BLOCKS + [state_shape, state_shape],
        scratch_shapes=[pltpu.VMEM((L, S5_BLOCKS, rows, LANES), BF16),
                        pair_rows(F32), pair_rows(F32), pair_rows(F32), pair_rows(F32),
                        pair_rows(BF16), pair_rows(BF16),
                        pltpu.VMEM((S5_PAIRS, n_states, 2 * SSM_STATE), F32),
                        pltpu.VMEM((S5_PAIRS, n_states, 2 * SSM_STATE), F32)],
        compiler_params=_cparams("arbitrary"),
        name="s5_scan",
    )(*([u] * S5_BLOCKS), pairs(s0_re), pairs(s0_im), *consts)
    unpair = lambda s: s.transpose(1, 0, 2).reshape(n_states, N_SSM_GROUPS, SSM_STATE)
    return y, unpair(sr), unpair(si)


KEY_NEG_INF = INT_MIN + 0x7FFFFF


def _key_to_float(x):
    return pltpu.bitcast(jnp.where(x < 0, x ^ jnp.int32(0x7FFFFFFF), x), F32)


def _threshold_of(x):
    return jnp.where(x <= KEY_NEG_INF, -jnp.inf, _key_to_float(x))


def _kth_largest(count_ge, shape, n_sel):
    def bit_step(i, x):
        cand = x + lax.shift_left(jnp.int32(1), 31 - i)
        return jnp.where(count_ge(_key_to_float(cand)) >= n_sel, cand, x)
    return _threshold_of(lax.fori_loop(0, 32, bit_step, jnp.full(shape, INT_MIN, jnp.int32)))


def _kth_largest_radix4(count_ge3, shape, n_sel):
    def step(i, x):
        one = lax.shift_left(jnp.int32(1), 30 - 2 * i)
        c1, c2, c3 = x + one, x + 2 * one, x + 3 * one
        n1, n2, n3 = count_ge3(_key_to_float(c1), _key_to_float(c2), _key_to_float(c3))
        return jnp.where(n3 >= n_sel, c3, jnp.where(n2 >= n_sel, c2, jnp.where(n1 >= n_sel, c1, x)))
    return _threshold_of(lax.fori_loop(0, 16, step, jnp.full(shape, INT_MIN, jnp.int32)))


def _rows_sum(x):
    r, c = x.shape
    return jnp.sum(jnp.sum(x.reshape(r // 8, 8, c), axis=0), axis=0, keepdims=True)


def _rows_max(x):
    r, c = x.shape
    return jnp.max(jnp.max(x.reshape(r // 8, 8, c), axis=0), axis=0, keepdims=True)


def _dsa_prompt_kernel(qt_ref, qit_ref, wit_ref, kb_ref, kib_ref, vt_ref, tri_ref, o_ref,
                       key_scr, bias_scr, acc_scr, *, tq, kbs, kba, n_sel):
    qtile = pl.program_id(1)
    n_keys = qtile * tq + tq
    nks = (n_keys + kbs - 1) // kbs
    nka = (n_keys + kba - 1) // kba

    def causal(j, kb):
        kpos = j * kb + lax.broadcasted_iota(jnp.int32, (kb, tq), 0)
        tpos = qtile * tq + lax.broadcasted_iota(jnp.int32, (kb, tq), 1)
        return kpos <= tpos

    qit = qit_ref[...]
    wi_w = jnp.concatenate([qit[h * IDX_DIM:(h + 1) * IDX_DIM] for h in range(N_IDX_HEADS)], axis=1)
    wit = wit_ref[...]

    def score_block(j, _):
        rows = pl.ds(pl.multiple_of(j * kbs, kbs), kbs)
        d = _dot(kib_ref[rows, :], wi_w)
        sc = jnp.zeros((kbs, tq), F32)
        for h in range(N_IDX_HEADS):
            sc += jnp.maximum(d[:, h * tq:(h + 1) * tq], 0.0) * wit[h:h + 1, :]
        key_scr[rows, :] = jnp.where(causal(j, kbs), sc, -jnp.inf)
        return 0
    lax.fori_loop(0, nks, score_block, 0)

    def count(pred):
        def body(j, acc):
            blk = key_scr[pl.ds(pl.multiple_of(j * kbs, kbs), kbs), :]
            hit = jnp.where(pred(blk), 1.0, 0.0)
            return acc + jnp.sum(hit.reshape(kbs // 64, 64, tq), axis=0)
        acc = lax.fori_loop(0, nks, body, jnp.zeros((64, tq), F32))
        return _rows_sum(acc)

    thr = _kth_largest(lambda cand: count(lambda blk: blk >= cand), (1, tq), n_sel)

    @pl.when(jnp.max(count(lambda blk: blk >= thr)) > n_sel)
    def _():
        room = n_sel - count(lambda blk: blk > thr)
        kbt = tri_ref.shape[0]

        def demote(j, tie_seen):
            rows = pl.ds(pl.multiple_of(j * kbt, kbt), kbt)
            key = key_scr[rows, :]
            tie = jnp.where(key == thr, 1.0, 0.0)
            rank = tie_seen + _dot(tri_ref[...], tie.astype(BF16))
            key_scr[rows, :] = jnp.where((tie > 0.0) & (rank > room), -jnp.inf, key)
            return tie_seen + _rows_sum(tie)
        lax.fori_loop(0, (n_keys + kbt - 1) // kbt, demote, jnp.zeros((1, tq), F32))

    qt = qt_ref[...]
    zero = jnp.zeros((HEAD_DIM, tq), BF16)
    w_qk = []
    for n in range(N_KV_HEADS):
        cols = []
        for g in range(2):
            h = 2 * n + g
            qh = qt[h * HEAD_DIM:(h + 1) * HEAD_DIM]
            cols.append(jnp.concatenate([qh, zero] if n % 2 == 0 else [zero, qh], axis=0))
        w_qk.append(jnp.concatenate(cols, axis=1))

    sub = min(128, kba)

    def scores(kblk, bias, n, i):
        r = slice(i * sub, (i + 1) * sub)
        s = _dot(kblk[r, (n // 2) * LANES:(n // 2 + 1) * LANES], w_qk[n])
        return s + jnp.concatenate([bias[r], bias[r]], axis=1)

    def max_block(j, carry):
        rows = pl.ds(pl.multiple_of(j * kba, kba), kba)
        bias = jnp.where((key_scr[rows, :] >= thr) & causal(j, kba), 0.0, NEG)
        bias_scr[rows, :] = bias
        kblk = kb_ref[rows, :]
        out = []
        for n in range(N_KV_HEADS):
            mx = carry[n]
            for i in range(kba // sub):
                s = scores(kblk, bias, n, i)
                mx = jnp.maximum(mx, jnp.max(s.reshape(sub // 8, 8, 2 * tq), axis=0))
            out.append(mx)
        return tuple(out)
    mx = lax.fori_loop(0, nka, max_block, (jnp.full((8, 2 * tq), NEG, F32),) * N_KV_HEADS)
    m_row = [jnp.max(m, axis=0, keepdims=True) for m in mx]

    acc_scr[...] = jnp.zeros(acc_scr.shape, F32)

    def attend_block(j, carry):
        rows = pl.ds(pl.multiple_of(j * kba, kba), kba)
        bias = bias_scr[rows, :]
        kblk = kb_ref[rows, :]
        out = []
        for n in range(N_KV_HEADS):
            ls = carry[n]
            pieces = []
            for i in range(kba // sub):
                p = jnp.exp(scores(kblk, bias, n, i) - m_row[n])
                ls = ls + jnp.sum(p.reshape(sub // 8, 8, 2 * tq), axis=0)
                pieces.append(p.astype(BF16))
            acc_scr[n] += _dot(vt_ref[n * HEAD_DIM:(n + 1) * HEAD_DIM, rows], jnp.concatenate(pieces, axis=0))
            out.append(ls)
        return tuple(out)
    ls = lax.fori_loop(0, nka, attend_block, (jnp.zeros((8, 2 * tq), F32),) * N_KV_HEADS)

    parts = []
    for n in range(N_KV_HEADS):
        o = acc_scr[n] / jnp.sum(ls[n], axis=0, keepdims=True)
        parts += [o[:, :tq], o[:, tq:]]
    o_ref[...] = jnp.concatenate(parts, axis=0).T


def _upper_tri(n):
    r = np.arange(n)
    return jnp.asarray(r[:, None] <= r[None, :], BF16)


def _lower_tri(n):
    r = np.arange(n)
    return jnp.asarray(r[:, None] >= r[None, :], BF16)


def _dsa_prompt(qt, qit, wit, kb, kib, vt, b, s_len, tq=128, kbs=512, kba=512, kbt=256):
    kbs, kba, kbt = min(kbs, s_len), min(kba, s_len), min(kbt, s_len)
    n_sel = min(TOPK_KEYS, s_len // 4)
    nq = s_len // tq
    qspec = lambda r: pl.BlockSpec((r, tq), lambda bi, qi_: (0, bi * nq + qi_))
    kspec = lambda w: pl.BlockSpec((s_len, w), lambda bi, qi_: (bi, 0))
    return pl.pallas_call(
        functools.partial(_dsa_prompt_kernel, tq=tq, kbs=kbs, kba=kba, n_sel=n_sel),
        grid=(b, nq),
        in_specs=[qspec(512), qspec(256), qspec(8), kspec(256), kspec(IDX_DIM),
                  pl.BlockSpec((256, s_len), lambda bi, qi_: (0, bi)),
                  pl.BlockSpec((kbt, kbt), lambda bi, qi_: (0, 0))],
        out_specs=pl.BlockSpec((tq, D_ATT), lambda bi, qi_: (bi * nq + qi_, 0)),
        out_shape=jax.ShapeDtypeStruct((b * s_len, D_ATT), F32),
        scratch_shapes=[pltpu.VMEM((s_len, tq), F32),
                        pltpu.VMEM((s_len, tq), F32),
                        pltpu.VMEM((N_KV_HEADS, HEAD_DIM, 2 * tq), F32)],
        compiler_params=_cparams("arbitrary", "arbitrary"),
        name="dsa_prompt",
    )(qt, qit, wit, kb, kib, vt, _lower_tri(kbt))


def _dsa_sample_kernel(pt_ref, q_ref, qis_ref, wi_ref, knew_ref, vnew_ref, kinew_ref, tri_ref,
                       ck_ref, cv_ref, cki_ref, o_ref,
                       kibuf, kbuf, vbuf, sc_scr, sem_ki, sem_kv,
                       *, n_pages, n_sel, kbi, kba, ts):
    b = pl.program_id(0)
    nb = pl.num_programs(0)
    slot = b % 2
    past = n_pages * PAGE_SIZE
    rows = N_HEADS * ts
    hrows = rows // N_KV_HEADS

    def page_rows(p):
        return pl.ds(pl.multiple_of(p * PAGE_SIZE, PAGE_SIZE), PAGE_SIZE)

    def ki_copy(bb, sl, p):
        return pltpu.make_async_copy(cki_ref.at[pt_ref[bb, p]], kibuf.at[sl, page_rows(p), :], sem_ki.at[sl])

    def k_copy(bb, p, n):
        return pltpu.make_async_copy(ck_ref.at[pt_ref[bb, p], :, n, :], kbuf.at[n, page_rows(p), :], sem_kv.at[0])

    def v_copy(bb, p, n):
        return pltpu.make_async_copy(cv_ref.at[pt_ref[bb, p], :, n, :], vbuf.at[n, page_rows(p), :], sem_kv.at[1])

    def for_pages(fn):
        def body(p, _):
            fn(p)
            return 0
        lax.fori_loop(0, n_pages, body, 0)

    def for_page_heads(fn):
        for_pages(lambda p: [fn(p, n) for n in range(N_KV_HEADS)])

    @pl.when(b == 0)
    def _():
        for_pages(lambda p: ki_copy(b, slot, p).start())

    def start_kv(p, n):
        k_copy(b, p, n).start()
        v_copy(b, p, n).start()
    for_page_heads(start_kv)

    @pl.when(b + 1 < nb)
    def _():
        for_pages(lambda p: ki_copy(b + 1, 1 - slot, p).start())

    for_pages(lambda p: ki_copy(b, slot, p).wait())

    qis = qis_ref[...]
    wi = wi_ref[...]
    n_tiles = past // LANES + 1

    def index_scores(kib):
        d = jnp.maximum(_dot_nt(qis, kib.astype(BF16)), 0.0)
        sc = jnp.zeros((ts, kib.shape[0]), F32)
        for h in range(N_IDX_HEADS):
            sc += d[h * ts:(h + 1) * ts] * wi[:, h:h + 1]
        return sc

    def score_block(j, _):
        cols = pl.ds(pl.multiple_of(j * kbi, kbi), kbi)
        sc_scr[:, cols] = index_scores(kibuf[slot, cols, :])
        return 0
    lax.fori_loop(0, past // kbi, score_block, 0)
    pad_rows = LANES - ts
    kin = jnp.concatenate([kinew_ref[...], jnp.zeros((pad_rows, IDX_DIM), F32)], axis=0)
    new_ok = (lax.broadcasted_iota(jnp.int32, (ts, LANES), 1)
              <= lax.broadcasted_iota(jnp.int32, (ts, LANES), 0))
    sc_scr[:, past:past + LANES] = jnp.where(new_ok, index_scores(kin), -jnp.inf)

    def counts(preds):
        sc = sc_scr[...]
        parts = [[jnp.zeros((ts, LANES), F32), jnp.zeros((ts, LANES), F32)] for _ in preds]
        for c in range(n_tiles):
            tile = sc[:, c * LANES:(c + 1) * LANES]
            for i, pred in enumerate(preds):
                parts[i][c % 2] = parts[i][c % 2] + jnp.where(pred(tile), 1.0, 0.0)
        return [jnp.sum(a + b_, axis=-1, keepdims=True) for a, b_ in parts]

    thr = _kth_largest_radix4(lambda c1, c2, c3: counts([lambda s: s >= c1, lambda s: s >= c2, lambda s: s >= c3]),
                              (ts, 1), n_sel)
    n_ge, n_gt = counts([lambda s: s >= thr, lambda s: s > thr])

    @pl.when(jnp.max(n_ge) > n_sel)
    def _():
        room = n_sel - n_gt
        kbt = tri_ref.shape[0]

        def demote(cols, tie_seen, tri):
            sc = sc_scr[:, cols]
            tie = jnp.where(sc == thr, 1.0, 0.0)
            rank = tie_seen + _dot(tie.astype(BF16), tri)
            sc_scr[:, cols] = jnp.where((tie > 0.0) & (rank > room), -jnp.inf, sc)
            return tie_seen + jnp.sum(tie, axis=-1, keepdims=True)
        seen = lax.fori_loop(0, past // kbt,
                             lambda j, s: demote(pl.ds(pl.multiple_of(j * kbt, kbt), kbt), s, tri_ref[...]),
                             jnp.zeros((ts, 1), F32))
        demote(pl.ds(past, LANES), seen, tri_ref[:LANES, :LANES])

    sc_scr[...] = jnp.where(sc_scr[...] >= thr, 0.0, NEG)
    sc_scr[:, past:past + LANES] = jnp.where(new_ok, sc_scr[:, past:past + LANES], NEG)

    for_page_heads(lambda p, n: k_copy(b, p, n).wait())
    for_page_heads(lambda p, n: v_copy(b, p, n).wait())

    q = q_ref[...]
    zpad = jnp.zeros((pad_rows, HEAD_DIM), F32)

    def key_block(j):
        return pl.ds(pl.multiple_of(j * kba, kba), kba)

    def head_rows(x, n):
        return x[n * hrows:(n + 1) * hrows]

    def scores(k_of_head, bias):
        s = jnp.concatenate([_dot_nt(head_rows(q, n), k_of_head(n).astype(BF16)) for n in range(N_KV_HEADS)],
                            axis=0)
        return s + jnp.concatenate([bias] * N_HEADS, axis=0)

    def block_scores(j):
        return scores(lambda n: kbuf[n, key_block(j), :], sc_scr[:, key_block(j)])

    def new_scores():
        return scores(lambda n: jnp.concatenate([knew_ref[n], zpad], axis=0), sc_scr[:, past:past + LANES])

    def lane_tiles(x, op, acc):
        for c in range(x.shape[1] // LANES):
            acc = op(acc, x[:, c * LANES:(c + 1) * LANES])
        return acc

    mx = lax.fori_loop(0, past // kba, lambda j, m: lane_tiles(block_scores(j), jnp.maximum, m),
                       jnp.full((rows, LANES), NEG, F32))
    m_row = jnp.max(jnp.maximum(mx, new_scores()), axis=-1, keepdims=True)

    def value_product(p, v_of_head, accs):
        return tuple(accs[n] + _dot(head_rows(p, n).astype(BF16), v_of_head(n).astype(BF16))
                     for n in range(N_KV_HEADS))

    def attend_block(j, carry):
        ls, accs = carry
        p = jnp.exp(block_scores(j) - m_row)
        accs = value_product(p, lambda n: vbuf[n, key_block(j), :], accs)
        return lane_tiles(p, jnp.add, ls), accs
    zero_acc = (jnp.zeros((hrows, HEAD_DIM), F32),) * N_KV_HEADS
    ls, accs = lax.fori_loop(0, past // kba, attend_block, (jnp.zeros((rows, LANES), F32), zero_acc))
    p_new = jnp.exp(new_scores() - m_row)
    accs = value_product(p_new, lambda n: jnp.concatenate([vnew_ref[n], zpad], axis=0), accs)
    l_row = jnp.sum(ls + p_new, axis=-1, keepdims=True)
    o_ref[...] = jnp.concatenate(accs, axis=0) / l_row


def _dsa_sample(q, qi, wi, k, v, ki, cache_k, cache_v, cache_kidx, page_table, ts, kbi=2048, kba=1024, kbt=512):
    bd, n_pages = page_table.shape
    past = n_pages * PAGE_SIZE
    kbi, kba, kbt = min(kbi, past), min(kba, past), min(kbt, past)
    n_sel = min(TOPK_KEYS, (past + ts) // 4)
    rows = N_HEADS * ts
    heads_first = lambda a, nh: a.reshape(bd, ts, nh, -1).transpose(0, 2, 1, 3)
    qh = heads_first(q, N_HEADS).reshape(bd, rows, HEAD_DIM)
    qis = heads_first(qi, N_IDX_HEADS).reshape(bd, N_IDX_HEADS * ts, IDX_DIM)
    per_b = lambda *s: pl.BlockSpec((None,) + s, lambda i, pt: (i,) + (0,) * len(s))
    any_spec = pl.BlockSpec(memory_space=pl.ANY)
    page_buf = pltpu.VMEM((N_KV_HEADS, past, HEAD_DIM), F32)
    o = pl.pallas_call(
        functools.partial(_dsa_sample_kernel, n_pages=n_pages, n_sel=n_sel, kbi=kbi, kba=kba, ts=ts),
        grid_spec=pltpu.PrefetchScalarGridSpec(
            num_scalar_prefetch=1,
            grid=(bd,),
            in_specs=[per_b(rows, HEAD_DIM), per_b(N_IDX_HEADS * ts, IDX_DIM), per_b(ts, N_IDX_HEADS),
                      per_b(N_KV_HEADS, ts, HEAD_DIM), per_b(N_KV_HEADS, ts, HEAD_DIM), per_b(ts, IDX_DIM),
                      pl.BlockSpec((kbt, kbt), lambda i, pt: (0, 0)),
                      any_spec, any_spec, any_spec],
            out_specs=per_b(rows, HEAD_DIM),
            scratch_shapes=[pltpu.VMEM((2, past, IDX_DIM), F32), page_buf, page_buf,
                            pltpu.VMEM((ts, past + LANES), F32),
                            pltpu.SemaphoreType.DMA((2,)),
                            pltpu.SemaphoreType.DMA((2,))]),
        out_shape=jax.ShapeDtypeStruct((bd, rows, HEAD_DIM), F32),
        compiler_params=_cparams("arbitrary"),
        name="dsa_sample",
    )(page_table, qh, qis, wi.reshape(bd, ts, N_IDX_HEADS), heads_first(k, N_KV_HEADS),
      heads_first(v, N_KV_HEADS), ki.reshape(bd, ts, IDX_DIM), _upper_tri(kbt),
      cache_k, cache_v, cache_kidx)
    return o.reshape(bd, N_HEADS, ts, HEAD_DIM).transpose(0, 2, 1, 3).reshape(bd * ts, D_ATT)


def _layer_norm(x, g, b):
    mu = jnp.mean(x, axis=-1, keepdims=True)
    xc = x - mu
    var = jnp.mean(xc * xc, axis=-1, keepdims=True)
    return xc * lax.rsqrt(var + LN_EPS) * g + b


def _rms_norm(x, g):
    return x * lax.rsqrt(jnp.mean(x * x, axis=-1, keepdims=True) + LN_EPS) * g


def _mix_kernel(*refs):
    ys_refs, refs = refs[:S5_BLOCKS], refs[S5_BLOCKS:]
    (ya_ref, x_ref, g1_ref, sc2_ref, sh2_ref, wglu_ref, bglu_ref, gs_ref, ga_ref,
     wout_ref, l1g_ref, l1b_ref, wr_ref, br_ref, tri_ref,
     x1_ref, h2_ref, eidx_ref, gate_ref, rank_ref, cnt_ref, run_scr) = refs
    g = jax.nn.gelu(jnp.concatenate([r[...] for r in ys_refs], axis=-1))
    ssm = g * jax.nn.sigmoid(_dot(g.astype(BF16), wglu_ref[...]) + bglu_ref[...])
    cat = jnp.concatenate([_rms_norm(ssm, gs_ref[...]), _rms_norm(ya_ref[...], ga_ref[...])], axis=-1)
    mixed = _dot(cat.astype(BF16), wout_ref[...])
    x1 = _layer_norm(ALPHA * x_ref[...] + g1_ref[...] * mixed, l1g_ref[...], l1b_ref[...])
    x1_ref[...] = x1
    h2 = (x1 * (1.0 + sc2_ref[...]) + sh2_ref[...]).astype(BF16)
    h2_ref[...] = h2
    aff = jax.nn.sigmoid(_dot(h2, wr_ref[...]))
    vals = aff + br_ref[...]
    tm = aff.shape[0]
    lane = lax.broadcasted_iota(jnp.int32, (tm, N_EXPERTS), 1)
    col = lax.broadcasted_iota(jnp.int32, (tm, 8), 1)
    eidx = jnp.zeros((tm, 8), jnp.int32)
    gsel = jnp.zeros((tm, 8), F32)
    chosen = jnp.zeros((tm, N_EXPERTS), F32)
    picks = []
    for j in range(TOP_K_EXPERTS):
        top = jnp.max(vals, axis=-1, keepdims=True)
        idx = jnp.min(jnp.where(vals == top, lane, N_EXPERTS), axis=-1, keepdims=True)
        hit = lane == idx
        gj = jnp.sum(jnp.where(hit, aff, 0.0), axis=-1, keepdims=True)
        eidx = jnp.where(col == j, idx, eidx)
        gsel = jnp.where(col == j, gj, gsel)
        vals = jnp.where(hit, -jnp.inf, vals)
        chosen = jnp.where(hit, 1.0, chosen)
        picks.append(idx)
    eidx_ref[...] = eidx
    gate_ref[...] = gsel / jnp.sum(gsel, axis=-1, keepdims=True) * ROUTED_SCALE

    @pl.when(pl.program_id(0) == 0)
    def _():
        run_scr[...] = jnp.zeros(run_scr.shape, F32)
    before = run_scr[...] + _dot(tri_ref[...], chosen.astype(BF16))
    rank = jnp.zeros((tm, 8), jnp.int32)
    for j in range(TOP_K_EXPERTS):
        rj = jnp.sum(jnp.where(lane == picks[j], before, 0.0), axis=-1, keepdims=True)
        rank = jnp.where(col == j, rj.astype(jnp.int32), rank)
    rank_ref[...] = rank
    run_scr[...] += jnp.sum(chosen, axis=0, keepdims=True)
    cnt_ref[...] = jnp.broadcast_to(run_scr[...], cnt_ref.shape)


def _mixer_out(ys, ya, x, g1, sc2, sh2, w, rows_per_mod, tm):
    t = x.shape[0]
    ms = _mod_spec(g1, tm, rows_per_mod)
    row = lambda wd: pl.BlockSpec((tm, wd), lambda i: (i, 0))
    full = lambda a: pl.BlockSpec(a.shape, lambda i: (0, 0))
    r = np.arange(tm)
    strict_lower = jnp.asarray(r[None, :] < r[:, None], BF16)
    consts = [w['w_glu'], w['b_glu'], w['g_ssm_out'], w['g_att_out'], w['w_out'], w['ln1_g'], w['ln1_b'],
              w['w_router'], w['b_router'], strict_lower]
    return pl.pallas_call(
        _mix_kernel,
        grid=(t // tm,),
        in_specs=[row(LANES)] * S5_BLOCKS + [row(D_ATT), row(D_MODEL), ms, ms, ms]
                 + [full(a) for a in consts],
        out_specs=[row(D_MODEL), row(D_MODEL), row(8), row(8), row(8),
                   pl.BlockSpec((8, N_EXPERTS), lambda i: (0, 0))],
        out_shape=[jax.ShapeDtypeStruct((t, D_MODEL), F32), jax.ShapeDtypeStruct((t, D_MODEL), BF16),
                   jax.ShapeDtypeStruct((t, 8), jnp.int32), jax.ShapeDtypeStruct((t, 8), F32),
                   jax.ShapeDtypeStruct((t, 8), jnp.int32), jax.ShapeDtypeStruct((8, N_EXPERTS), F32)],
        scratch_shapes=[pltpu.VMEM((1, N_EXPERTS), F32)],
        compiler_params=_cparams("arbitrary"),
        name="mixer_out",
    )(*ys, ya, x, g1, sc2, sh2, *consts)


MOE_ROWS = 256


def _experts_kernel(be_ref, nused_ref, x_ref, w1_ref, w3_ref, w2_ref, y_ref):
    i = pl.program_id(0)

    @pl.when(i < nused_ref[0])
    def _():
        x = x_ref[...]
        h1 = _dot(x, w1_ref[...].astype(BF16))
        h3 = _dot(x, w3_ref[...].astype(BF16))
        a = (h1 * jax.nn.sigmoid(h1) * h3).astype(BF16)
        y_ref[...] = _dot(a, w2_ref[...].astype(BF16))

    @pl.when(i >= nused_ref[0])
    def _():
        y_ref[...] = jnp.zeros(y_ref.shape, F32)


def _experts(xs, block_e, n_used, w_e1, w_e3, w_e2):
    n_slots = xs.shape[0]
    n_blocks = n_slots // MOE_ROWS
    wspec = lambda r, c: pl.BlockSpec((None, r, c), lambda i, be, nu: (be[i], 0, 0))
    return pl.pallas_call(
        _experts_kernel,
        grid_spec=pltpu.PrefetchScalarGridSpec(
            num_scalar_prefetch=2,
            grid=(n_blocks,),
            in_specs=[pl.BlockSpec((MOE_ROWS, D_MODEL), lambda i, be, nu: (i, 0)),
                      wspec(D_MODEL, D_EXPERT), wspec(D_MODEL, D_EXPERT), wspec(D_EXPERT, D_MODEL)],
            out_specs=pl.BlockSpec((MOE_ROWS, D_MODEL), lambda i, be, nu: (i, 0))),
        out_shape=jax.ShapeDtypeStruct((n_slots, D_MODEL), F32),
        compiler_params=_cparams("arbitrary"),
        name="routed_experts",
    )(block_e, n_used, xs, w_e1, w_e3, w_e2)


def _dispatch_plan(eidx, rank, counts, n_tok):
    padded = (counts + MOE_ROWS - 1) // MOE_ROWS * MOE_ROWS
    pends = jnp.cumsum(padded)
    pstarts = pends - padded
    slot = pstarts[eidx] + rank
    n_blocks = -(-n_tok * TOP_K_EXPERTS // MOE_ROWS) + N_EXPERTS
    first_row = jnp.arange(n_blocks, dtype=jnp.int32) * MOE_ROWS
    block_e = jnp.sum((pends[None, :] <= first_row[:, None]).astype(jnp.int32), axis=1)
    block_e = jnp.minimum(block_e, N_EXPERTS - 1)
    n_used = (pends[-1:] // MOE_ROWS).astype(jnp.int32)
    return slot, block_e, n_used, n_blocks * MOE_ROWS


def _final_kernel(*refs):
    ye_refs = refs[:TOP_K_EXPERTS]
    gate_ref, h2_ref, x1_ref, g2_ref, ws1_ref, ws3_ref, ws2_ref, l2g_ref, l2b_ref, o_ref = refs[TOP_K_EXPERTS:]
    h2 = h2_ref[...]
    h1 = _dot(h2, ws1_ref[...])
    h3 = _dot(h2, ws3_ref[...])
    a = (h1 * jax.nn.sigmoid(h1) * h3).astype(BF16)
    gate = gate_ref[...]
    routed = ye_refs[0][...] * gate[:, 0:1]
    for j in range(1, TOP_K_EXPERTS):
        routed += ye_refs[j][...] * gate[:, j:j + 1]
    ffn = routed + _dot(a, ws2_ref[...])
    o_ref[...] = _layer_norm(ALPHA * x1_ref[...] + g2_ref[...] * ffn, l2g_ref[...], l2b_ref[...])


def _final(ye_parts, gates, h2, x1, g2, w, rows_per_mod, tm):
    t = x1.shape[0]
    row = lambda wd: pl.BlockSpec((tm, wd), lambda i: (i, 0))
    full = lambda a: pl.BlockSpec(a.shape, lambda i: (0, 0))
    consts = [w['w_s1'], w['w_s3'], w['w_s2'], w['ln2_g'], w['ln2_b']]
    return pl.pallas_call(
        _final_kernel,
        grid=(t // tm,),
        in_specs=[row(D_MODEL)] * TOP_K_EXPERTS
                 + [row(8), row(D_MODEL), row(D_MODEL), _mod_spec(g2, tm, rows_per_mod)]
                 + [full(a) for a in consts],
        out_specs=row(D_MODEL),
        out_shape=jax.ShapeDtypeStruct((t, D_MODEL), F32),
        compiler_params=_cparams("arbitrary"),
        name="final_residual",
    )(*ye_parts, gates, h2, x1, g2, *consts)


def kernel(x_prompt, x_sample, c_prompt, c_sample, cache_k, cache_v, cache_kidx, state_ssm_re, state_ssm_im, page_table, w_ada, b_ada, w_in, ssm_a_re, ssm_a_im, ssm_log_dt, ssm_b_re, ssm_b_im, ssm_c_re, ssm_c_im, ssm_d, w_glu, b_glu, g_ssm_out, g_att_out, w_out, ln1_g, ln1_b, w_router, b_router, w_e1, w_e3, w_e2, w_s1, w_s3, w_s2, ln2_g, ln2_b):
    assert w_in.shape[0] == 1, "one layer"
    b, s, d = x_prompt.shape
    bd, ts, _ = x_sample.shape
    tp, tsn = b * s, bd * ts
    tm = 512
    row = lambda a: a[0].reshape(1, -1)
    w = dict(w_glu=w_glu[0].astype(BF16), b_glu=row(b_glu), g_ssm_out=row(g_ssm_out), g_att_out=row(g_att_out),
             w_out=w_out[0].astype(BF16), ln1_g=row(ln1_g), ln1_b=row(ln1_b),
             w_router=w_router[0].astype(BF16), b_router=row(b_router),
             w_s1=w_s1[0].astype(BF16), w_s3=w_s3[0].astype(BF16), w_s2=w_s2[0].astype(BF16),
             ln2_g=row(ln2_g), ln2_b=row(ln2_b))
    w_sample, w_prompt, w_prompt_t = _projection_weights(w_in[0])
    ssm_args = (ssm_a_re[0], ssm_a_im[0], ssm_log_dt[0], ssm_b_re[0], ssm_b_im[0], ssm_c_re[0], ssm_c_im[0],
                ssm_d[0])

    mod = _ada_mod(jnp.concatenate([c_prompt, c_sample], axis=0), w_ada[0], b_ada[0])
    mod_p = mod[:b].reshape(b, 6, 1, d)
    mod_s = jnp.broadcast_to(mod[b:].reshape(bd, 1, 6, d), (bd, ts, 6, d)).reshape(tsn, 6, d)

    xp = x_prompt.reshape(tp, d)
    u, k_p, v_p, ki_p, kb, kib, qt, qit, vt, wit = _in_projection_prompt(
        xp, mod_p[:, 1], mod_p[:, 0], w_prompt, w_prompt_t, s, tm)
    mats = _ssm_matrices(*ssm_args)
    zero_state = jnp.zeros((b, N_SSM_GROUPS, SSM_STATE), F32)
    ys_p, re_p, im_p = _s5_scan(u.reshape(b, s, D_SSM), zero_state, zero_state, mats, True, min(s, 128))
    ya_p = _dsa_prompt(qt, qit, wit, kb, kib, vt, b, s)
    x1_p, h2_p, eidx_p, gate_p, rank_p, cnt_p = _mixer_out([y.reshape(tp, LANES) for y in ys_p], ya_p, xp,
                                                           mod_p[:, 2], mod_p[:, 4], mod_p[:, 3], w, s, tm)

    assert ts == S5_CHUNK, "each sample sequence is one S5 chunk"
    xs = x_sample.reshape(tsn, d)
    tms = min(tm, tsn)
    u, q, k_s, v_s, qi, ki_s, wi = _in_projection(xs, mod_s[:, 1], mod_s[:, 0], w_sample, None, tms)
    ys_s, re_s, im_s = _s5_scan(u.reshape(1, tsn, D_SSM), state_ssm_re[0], state_ssm_im[0], mats, False, tsn)
    ya_s = _dsa_sample(q, qi, wi, k_s, v_s, ki_s, cache_k[0], cache_v[0], cache_kidx[0], page_table, ts)
    x1_s, h2_s, eidx_s, gate_s, rank_s, cnt_s = _mixer_out([y.reshape(tsn, LANES) for y in ys_s], ya_s, xs,
                                                           mod_s[:, 2], mod_s[:, 4], mod_s[:, 3], w, None, tms)

    n_tok = tp + tsn
    cnt_p, cnt_s = cnt_p[0].astype(jnp.int32), cnt_s[0].astype(jnp.int32)
    eidx_p, eidx_s = eidx_p[:, :TOP_K_EXPERTS], eidx_s[:, :TOP_K_EXPERTS]
    eidx = jnp.concatenate([eidx_p, eidx_s], axis=0)
    rank = jnp.concatenate([rank_p[:, :TOP_K_EXPERTS], rank_s[:, :TOP_K_EXPERTS] + cnt_p[eidx_s]], axis=0)
    slot, block_e, n_used, n_slots = _dispatch_plan(eidx, rank, cnt_p + cnt_s, n_tok)
    tok = jnp.broadcast_to(jnp.arange(n_tok, dtype=jnp.int32)[:, None], slot.shape)
    slot_tok = jnp.zeros((n_slots,), jnp.int32).at[slot.reshape(-1)].set(tok.reshape(-1))
    h2 = jnp.concatenate([h2_p, h2_s], axis=0)
    ye = _experts(h2[slot_tok], block_e, n_used, w_e1[0], w_e3[0], w_e2[0])
    parts_p = [ye[slot[:tp, j]] for j in range(TOP_K_EXPERTS)]
    parts_s = [ye[slot[tp:, j]] for j in range(TOP_K_EXPERTS)]

    y_p = _final(parts_p, gate_p, h2_p, x1_p, mod_p[:, 5], w, s, tm)
    y_s = _final(parts_s, gate_s, h2_s, x1_s, mod_s[:, 5], w, None, tms)

    kvs = (1, b, s, N_KV_HEADS, HEAD_DIM)
    kvd = (1, bd, ts, N_KV_HEADS, HEAD_DIM)
    return (y_p.reshape(b, s, d), y_s.reshape(bd, ts, d),
            k_p.reshape(kvs), v_p.reshape(kvs), ki_p.reshape(1, b, s, IDX_DIM), re_p[None], im_p[None],
            k_s.reshape(kvd), v_s.reshape(kvd), ki_s.reshape(1, bd, ts, IDX_DIM), re_s[None], im_s[None])
```

```python
import functools
import math

import jax
import jax.numpy as jnp
import numpy as np
from jax import lax
from jax.experimental import pallas as pl
from jax.experimental.pallas import tpu as pltpu

D_MODEL = 1024
D_SSM = 512
SSM_GROUP = 16
N_SSM_GROUPS = 32
SSM_STATE = 64
D_ATT = 512
HEAD_DIM = 64
N_HEADS = 8
N_KV_HEADS = 4
N_IDX_HEADS = 4
IDX_DIM = 64
TOPK_KEYS = 256
N_EXPERTS = 64
TOP_K_EXPERTS = 6
D_EXPERT = 256
ROUTED_SCALE = 2.5
PAGE_SIZE = 128
ALPHA = 2.0 ** 0.25
LN_EPS = 1e-5

LANES = 128
VMEM_LIMIT = 56 * 1024 * 1024
NEG = -1e30
INT_MIN = -2 ** 31

F32 = jnp.float32
BF16 = jnp.bfloat16
HI = lax.Precision.HIGHEST


def _cparams(*sem):
    return pltpu.CompilerParams(dimension_semantics=sem, vmem_limit_bytes=VMEM_LIMIT)


def _dot(a, b):
    return jnp.dot(a, b, preferred_element_type=F32)


def _dot_nt(a, b):
    return lax.dot_general(a, b, (((1,), (1,)), ((), ())), preferred_element_type=F32)


def _ada_kernel(c_ref, w_ref, b_ref, o_ref):
    c = c_ref[...]
    s = (c * jax.nn.sigmoid(c)).astype(BF16)
    o_ref[...] = _dot(s, w_ref[...].astype(BF16)) + b_ref[...]


def _ada_mod(c, w_ada, b_ada):
    n, d = c.shape
    dn = w_ada.shape[1]
    tn = 1024
    return pl.pallas_call(
        _ada_kernel,
        grid=(dn // tn,),
        in_specs=[pl.BlockSpec((n, d), lambda j: (0, 0)),
                  pl.BlockSpec((d, tn), lambda j: (0, j)),
                  pl.BlockSpec((1, tn), lambda j: (0, j))],
        out_specs=pl.BlockSpec((n, tn), lambda j: (0, j)),
        out_shape=jax.ShapeDtypeStruct((n, dn), F32),
        compiler_params=_cparams("arbitrary"),
        name="ada_mod",
    )(c, w_ada, b_ada.reshape(1, dn))


_C_U, _C_Q, _C_K, _C_V, _C_QI, _C_KI, _C_WI, _C_END = 0, 512, 1024, 1280, 1536, 1792, 1920, 2048
_P_U, _P_K, _P_V, _P_KI, _P_END = 0, 512, 768, 1024, 1152
_T_Q, _T_QI, _T_V, _T_WI, _T_END = 0, 512, 768, 1024, 1032


def _projection_weights(w_in):
    d = w_in.shape[0]
    cuts = np.cumsum([D_SSM, N_HEADS * HEAD_DIM, N_KV_HEADS * HEAD_DIM, N_KV_HEADS * HEAD_DIM,
                      N_IDX_HEADS * IDX_DIM, IDX_DIM])
    wu, wq, wk, wv, wqi, wki, wwi = jnp.split(w_in, cuts.tolist(), axis=1)
    wq, wqi, wwi = wq * HEAD_DIM ** -0.5, wqi * IDX_DIM ** -0.5, wwi * N_IDX_HEADS ** -0.5
    z = lambda n: jnp.zeros((d, n), w_in.dtype)
    sample = jnp.concatenate([wu, wq, wk, wv, wqi, wki, z(64), wwi, z(124)], axis=1)
    prompt = jnp.concatenate([wu, wk, wv, wki, z(64)], axis=1)
    prompt_t = jnp.concatenate([wq, wqi, wv, wwi, z(4)], axis=1).T
    return sample.astype(BF16), prompt.astype(BF16), prompt_t.astype(BF16)


def _inproj_kernel(x_ref, sc_ref, sh_ref, w_ref, u_ref, q_ref, k_ref, v_ref, qi_ref, ki_ref, wi_ref):
    h = (x_ref[...] * (1.0 + sc_ref[...]) + sh_ref[...]).astype(BF16)
    r = _dot(h, w_ref[...])
    u_ref[...] = r[:, _C_U:_C_Q]
    q_ref[...] = r[:, _C_Q:_C_K].astype(BF16)
    k_ref[...] = r[:, _C_K:_C_V]
    v_ref[...] = r[:, _C_V:_C_QI]
    qi_ref[...] = r[:, _C_QI:_C_KI].astype(BF16)
    ki_ref[...] = r[:, _C_KI:_C_KI + IDX_DIM]
    wi_ref[...] = r[:, _C_WI:_C_WI + N_IDX_HEADS]


def _inproj_prompt_kernel(x_ref, sc_ref, sh_ref, w_ref, wt_ref,
                          u_ref, k_ref, v_ref, ki_ref, kb_ref, kib_ref, qt_ref, qit_ref, vt_ref, wit_ref):
    h = (x_ref[...] * (1.0 + sc_ref[...]) + sh_ref[...]).astype(BF16)
    r = _dot(h, w_ref[...])
    u_ref[...] = r[:, _P_U:_P_K]
    k = r[:, _P_K:_P_V]
    k_ref[...] = k
    kb_ref[...] = k.astype(BF16)
    v_ref[...] = r[:, _P_V:_P_KI]
    ki = r[:, _P_KI:_P_KI + IDX_DIM]
    ki_ref[...] = ki
    kib_ref[...] = ki.astype(BF16)
    rt = _dot_nt(wt_ref[...], h)
    qt_ref[...] = rt[_T_Q:_T_QI].astype(BF16)
    qit_ref[...] = rt[_T_QI:_T_V].astype(BF16)
    vt_ref[...] = rt[_T_V:_T_WI].astype(BF16)
    wit_ref[...] = rt[_T_WI:_T_END]


def _mod_spec(mod, tm, rows_per_mod):
    if rows_per_mod is None:
        return pl.BlockSpec((tm, D_MODEL), lambda i: (i, 0))
    tiles = rows_per_mod // tm
    return pl.BlockSpec((None, 1, D_MODEL), lambda i: (i // tiles, 0, 0))


def _in_projection(x, sc, sh, w_tok, rows_per_mod, tm):
    t = x.shape[0]
    widths = [(D_SSM, F32), (512, BF16), (256, F32), (256, F32), (256, BF16), (IDX_DIM, F32),
              (N_IDX_HEADS, F32)]
    ms = _mod_spec(sc, tm, rows_per_mod)
    return pl.pallas_call(
        _inproj_kernel,
        grid=(t // tm,),
        in_specs=[pl.BlockSpec((tm, D_MODEL), lambda i: (i, 0)), ms, ms,
                  pl.BlockSpec(w_tok.shape, lambda i: (0, 0))],
        out_specs=[pl.BlockSpec((tm, w), lambda i: (i, 0)) for w, _ in widths],
        out_shape=[jax.ShapeDtypeStruct((t, w), dt) for w, dt in widths],
        compiler_params=_cparams("arbitrary"),
        name="in_projection",
    )(x, sc, sh, w_tok)


def _in_projection_prompt(x, sc, sh, w_tok, w_feat, rows_per_mod, tm):
    t = x.shape[0]
    tok = [(D_SSM, F32), (256, F32), (256, F32), (IDX_DIM, F32), (256, BF16), (IDX_DIM, BF16)]
    feat = [(512, BF16), (256, BF16), (256, BF16), (8, F32)]
    ms = _mod_spec(sc, tm, rows_per_mod)
    return pl.pallas_call(
        _inproj_prompt_kernel,
        grid=(t // tm,),
        in_specs=[pl.BlockSpec((tm, D_MODEL), lambda i: (i, 0)), ms, ms,
                  pl.BlockSpec(w_tok.shape, lambda i: (0, 0)), pl.BlockSpec(w_feat.shape, lambda i: (0, 0))],
        out_specs=[pl.BlockSpec((tm, w), lambda i: (i, 0)) for w, _ in tok]
                  + [pl.BlockSpec((r, tm), lambda i: (0, i)) for r, _ in feat],
        out_shape=[jax.ShapeDtypeStruct((t, w), dt) for w, dt in tok]
                  + [jax.ShapeDtypeStruct((r, t), dt) for r, dt in feat],
        compiler_params=_cparams("arbitrary"),
        name="in_projection_prompt",
    )(x, sc, sh, w_tok, w_feat)


S5_CHUNK = 8
S5_BLOCKS = D_SSM // LANES
S5_GROUPS_PER_BLOCK = LANES // SSM_GROUP
S5_PAIRS = N_SSM_GROUPS // 2


def _ssm_matrices(a_re, a_im, log_dt, b_re, b_im, c_re, c_im, d_skip):
    g, p = a_re.shape
    L = S5_CHUNK
    dt = jnp.exp(log_dt)[:, None]
    mag = jnp.exp(dt * a_re)
    abr, abi = mag * jnp.cos(dt * a_im), mag * jnp.sin(dt * a_im)
    den = a_re * a_re + a_im * a_im
    nr, ni = abr - 1.0, abi
    fr = (nr * a_re + ni * a_im) / den
    fi = (ni * a_re - nr * a_im) / den
    bbr = fr[..., None] * b_re - fi[..., None] * b_im
    bbi = fr[..., None] * b_im + fi[..., None] * b_re
    dd = jnp.arange(L + 1, dtype=F32)[:, None, None]
    pm = jnp.exp(dd * (dt * a_re)[None])
    pw_r = pm * jnp.cos(dd * (dt * a_im)[None])
    pw_i = pm * jnp.sin(dd * (dt * a_im)[None])
    ca_r = c_re[None] * pw_r[:, :, None, :] - c_im[None] * pw_i[:, :, None, :]
    ca_i = c_re[None] * pw_i[:, :, None, :] + c_im[None] * pw_r[:, :, None, :]
    kd = (jnp.einsum('dgop,gpi->dgoi', ca_r[:L], bbr, precision=HI)
          - jnp.einsum('dgop,gpi->dgoi', ca_i[:L], bbi, precision=HI))
    nblk, gblk, npair = S5_BLOCKS, S5_GROUPS_PER_BLOCK, S5_PAIRS
    ppb = npair // nblk
    kd_t = jnp.transpose(kd, (1, 0, 3, 2)).reshape(nblk, gblk, L, SSM_GROUP, SSM_GROUP)
    mtd = jnp.einsum('qgdio,gh->qdgiho', kd_t, jnp.eye(gblk, dtype=F32))
    mtd = mtd.reshape(nblk, L, LANES, LANES)
    sel = np.zeros((ppb, gblk, 2), np.float32)
    for r in range(ppb):
        for t in range(2):
            sel[r, 2 * r + t, t] = 1.0
    sel = jnp.asarray(sel)
    rev_r, rev_i = pw_r[:L][::-1], pw_i[:L][::-1]
    f_r = rev_r[..., None] * bbr[None] - rev_i[..., None] * bbi[None]
    f_i = rev_r[..., None] * bbi[None] + rev_i[..., None] * bbr[None]

    def place_in(f):
        f = f.reshape(L, nblk, ppb, 2, p, SSM_GROUP)
        f = jnp.einsum('mqrtPi,rgt->qrmgitP', f, sel)
        return f.reshape(npair, L, LANES, 2 * p).astype(BF16)

    def place_out(c):
        c = c.reshape(L, nblk, ppb, 2, SSM_GROUP, p)
        c = jnp.einsum('jqrtoP,rgt->qrjtPgo', c, sel)
        return c.reshape(npair, L, 2 * p, LANES).astype(BF16)

    return dict(mtd=mtd.astype(BF16), mf_r=place_in(f_r), mf_i=place_in(f_i),
                mg_r=place_out(ca_r[1:]), mg_i=place_out(-ca_i[1:]),
                al_r=pw_r[L].reshape(npair, 1, 2 * p), al_i=pw_i[L].reshape(npair, 1, 2 * p),
                dsk=d_skip.reshape(nblk, 1, LANES))


def _s5_kernel(*refs, nseq, nc, chained):
    u_refs, refs = refs[:S5_BLOCKS], refs[S5_BLOCKS:]
    (s0r_ref, s0i_ref, mtd_ref, mfr_ref, mfi_ref, mgr_ref, mgi_ref, alr_ref, ali_ref, dsk_ref), refs = \
        refs[:10], refs[10:]
    y_refs, refs = refs[:S5_BLOCKS], refs[S5_BLOCKS:]
    sr_ref, si_ref, xb_scr, inr_scr, ini_scr, str_scr, sti_scr, sbr_scr, sbi_scr, car_r, car_i = refs
    L = S5_CHUNK
    rows = nseq * nc
    ppb = S5_PAIRS // S5_BLOCKS

    def x_block(m, q):
        return u_refs[q][:, pl.ds(m, nc, stride=L), :].reshape(rows, LANES)

    for m in range(L):
        for q in range(S5_BLOCKS):
            xb_scr[m, q] = x_block(m, q).astype(BF16)

    for p in range(S5_PAIRS):
        q = p // ppb
        acc_r = _dot(xb_scr[0, q], mfr_ref[p, 0])
        acc_i = _dot(xb_scr[0, q], mfi_ref[p, 0])
        for m in range(1, L):
            acc_r += _dot(xb_scr[m, q], mfr_ref[p, m])
            acc_i += _dot(xb_scr[m, q], mfi_ref[p, m])
        inr_scr[p] = acc_r
        ini_scr[p] = acc_i

    if chained:
        @pl.when(pl.program_id(0) == 0)
        def _():
            car_r[...] = s0r_ref[...]
            car_i[...] = s0i_ref[...]

        def step(c, carry):
            idx = pl.ds(c, nseq, stride=nc)
            out = []
            for p in range(S5_PAIRS):
                sr, si = carry[2 * p], carry[2 * p + 1]
                str_scr[p, idx, :] = sr
                sti_scr[p, idx, :] = si
                alr, ali = alr_ref[p], ali_ref[p]
                out.append(alr * sr - ali * si + inr_scr[p, idx, :])
                out.append(alr * si + ali * sr + ini_scr[p, idx, :])
            return tuple(out)

        init = []
        for p in range(S5_PAIRS):
            init += [car_r[p], car_i[p]]
        end = lax.fori_loop(0, nc, step, tuple(init))
        for p in range(S5_PAIRS):
            car_r[p] = end[2 * p]
            car_i[p] = end[2 * p + 1]
        sr_ref[...] = car_r[...]
        si_ref[...] = car_i[...]
    else:
        for p in range(S5_PAIRS):
            sr, si = s0r_ref[p], s0i_ref[p]
            str_scr[p] = sr
            sti_scr[p] = si
            alr, ali = alr_ref[p], ali_ref[p]
            sr_ref[p] = alr * sr - ali * si + inr_scr[p]
            si_ref[p] = alr * si + ali * sr + ini_scr[p]

    for p in range(S5_PAIRS):
        sbr_scr[p] = str_scr[p].astype(BF16)
        sbi_scr[p] = sti_scr[p].astype(BF16)

    for q in range(S5_BLOCKS):
        for j in range(L):
            acc = _dot(xb_scr[j, q], mtd_ref[q, 0])
            for m in range(j):
                acc += _dot(xb_scr[m, q], mtd_ref[q, j - m])
            for r in range(ppb):
                p = q * ppb + r
                acc += _dot(sbr_scr[p], mgr_ref[p, j])
                acc += _dot(sbi_scr[p], mgi_ref[p, j])
            y = acc + dsk_ref[q] * x_block(j, q)
            y_refs[q][:, pl.ds(j, nc, stride=L), :] = y.reshape(nseq, nc, LANES)


def _s5_scan(u, s0_re, s0_im, mats, chained, tt):
    nseq, t, _ = u.shape
    n_states = s0_re.shape[0]
    L = S5_CHUNK
    nc = tt // L
    rows = nseq * nc
    assert rows == n_states or chained
    pairs = lambda s: s.reshape(n_states, S5_PAIRS, 2 * SSM_STATE).transpose(1, 0, 2)
    full = lambda a: pl.BlockSpec(a.shape, lambda i: (0,) * a.ndim)
    consts = [mats[k] for k in ('mtd', 'mf_r', 'mf_i', 'mg_r', 'mg_i', 'al_r', 'al_i', 'dsk')]
    state_shape = jax.ShapeDtypeStruct((S5_PAIRS, n_states, 2 * SSM_STATE), F32)
    state_spec = pl.BlockSpec((S5_PAIRS, n_states, 2 * SSM_STATE), lambda i: (0, 0, 0))
    pair_rows = lambda dt: pltpu.VMEM((S5_PAIRS, rows, LANES), dt)
    lane_block = lambda q: pl.BlockSpec((nseq, tt, LANES), lambda i: (0, i, q))
    out_block = pl.BlockSpec((nseq, tt, LANES), lambda i: (0, i, 0))
    *y, sr, si = pl.pallas_call(
        functools.partial(_s5_kernel, nseq=nseq, nc=nc, chained=chained),
        grid=(t // tt,),
        in_specs=[lane_block(q) for q in range(S5_BLOCKS)] + [state_spec, state_spec]
                 + [full(a) for a in consts],
        out_specs=[out_block] * S5_BLOCKS + [state_spec, state_spec],
        out_shape=[jax.ShapeDtypeStruct((nseq, t, LANES), F32)] * S5_BLOCKS + [state_shape, state_shape],
        scratch_shapes=[pltpu.VMEM((L, S5_BLOCKS, rows, LANES), BF16),
                        pair_rows(F32), pair_rows(F32), pair_rows(F32), pair_rows(F32),
                        pair_rows(BF16), pair_rows(BF16),
                        pltpu.VMEM((S5_PAIRS, n_states, 2 * SSM_STATE), F32),
                        pltpu.VMEM((S5_PAIRS, n_states, 2 * SSM_STATE), F32)],
        compiler_params=_cparams("arbitrary"),
        name="s5_scan",
    )(*([u] * S5_BLOCKS), pairs(s0_re), pairs(s0_im), *consts)
    unpair = lambda s: s.transpose(1, 0, 2).reshape(n_states, N_SSM_GROUPS, SSM_STATE)
    return y, unpair(sr), unpair(si)


KEY_NEG_INF = INT_MIN + 0x7FFFFF


def _key_to_float(x):
    return pltpu.bitcast(jnp.where(x < 0, x ^ jnp.int32(0x7FFFFFFF), x), F32)


def _threshold_of(x):
    return jnp.where(x <= KEY_NEG_INF, -jnp.inf, _key_to_float(x))


def _kth_largest(count_ge, shape, n_sel):
    def bit_step(i, x):
        cand = x + lax.shift_left(jnp.int32(1), 31 - i)
        return jnp.where(count_ge(_key_to_float(cand)) >= n_sel, cand, x)
    return _threshold_of(lax.fori_loop(0, 32, bit_step, jnp.full(shape, INT_MIN, jnp.int32)))


def _kth_largest_radix4(count_ge3, shape, n_sel):
    def step(i, x):
        one = lax.shift_left(jnp.int32(1), 30 - 2 * i)
        c1, c2, c3 = x + one, x + 2 * one, x + 3 * one
        n1, n2, n3 = count_ge3(_key_to_float(c1), _key_to_float(c2), _key_to_float(c3))
        return jnp.where(n3 >= n_sel, c3, jnp.where(n2 >= n_sel, c2, jnp.where(n1 >= n_sel, c1, x)))
    return _threshold_of(lax.fori_loop(0, 16, step, jnp.full(shape, INT_MIN, jnp.int32)))


def _rows_sum(x):
    r, c = x.shape
    return jnp.sum(jnp.sum(x.reshape(r // 8, 8, c), axis=0), axis=0, keepdims=True)


def _rows_max(x):
    r, c = x.shape
    return jnp.max(jnp.max(x.reshape(r // 8, 8, c), axis=0), axis=0, keepdims=True)


def _dsa_prompt_kernel(qt_ref, qit_ref, wit_ref, kb_ref, kib_ref, vt_ref, tri_ref, o_ref,
                       key_scr, bias_scr, acc_scr, *, tq, kbs, kba, n_sel):
    qtile = pl.program_id(1)
    n_keys = qtile * tq + tq
    nks = (n_keys + kbs - 1) // kbs
    nka = (n_keys + kba - 1) // kba

    def causal(j, kb):
        kpos = j * kb + lax.broadcasted_iota(jnp.int32, (kb, tq), 0)
        tpos = qtile * tq + lax.broadcasted_iota(jnp.int32, (kb, tq), 1)
        return kpos <= tpos

    qit = qit_ref[...]
    wi_w = jnp.concatenate([qit[h * IDX_DIM:(h + 1) * IDX_DIM] for h in range(N_IDX_HEADS)], axis=1)
    wit = wit_ref[...]

    def score_block(j, _):
        rows = pl.ds(pl.multiple_of(j * kbs, kbs), kbs)
        d = _dot(kib_ref[rows, :], wi_w)
        sc = jnp.zeros((kbs, tq), F32)
        for h in range(N_IDX_HEADS):
            sc += jnp.maximum(d[:, h * tq:(h + 1) * tq], 0.0) * wit[h:h + 1, :]
        key_scr[rows, :] = jnp.where(causal(j, kbs), sc, -jnp.inf)
        return 0
    lax.fori_loop(0, nks, score_block, 0)

    def count(pred):
        def body(j, acc):
            blk = key_scr[pl.ds(pl.multiple_of(j * kbs, kbs), kbs), :]
            hit = jnp.where(pred(blk), 1.0, 0.0)
            return acc + jnp.sum(hit.reshape(kbs // 64, 64, tq), axis=0)
        acc = lax.fori_loop(0, nks, body, jnp.zeros((64, tq), F32))
        return _rows_sum(acc)

    thr = _kth_largest(lambda cand: count(lambda blk: blk >= cand), (1, tq), n_sel)

    @pl.when(jnp.max(count(lambda blk: blk >= thr)) > n_sel)
    def _():
        room = n_sel - count(lambda blk: blk > thr)
        kbt = tri_ref.shape[0]

        def demote(j, tie_seen):
            rows = pl.ds(pl.multiple_of(j * kbt, kbt), kbt)
            key = key_scr[rows, :]
            tie = jnp.where(key == thr, 1.0, 0.0)
            rank = tie_seen + _dot(tri_ref[...], tie.astype(BF16))
            key_scr[rows, :] = jnp.where((tie > 0.0) & (rank > room), -jnp.inf, key)
            return tie_seen + _rows_sum(tie)
        lax.fori_loop(0, (n_keys + kbt - 1) // kbt, demote, jnp.zeros((1, tq), F32))

    qt = qt_ref[...]
    zero = jnp.zeros((HEAD_DIM, tq), BF16)
    w_qk = []
    for n in range(N_KV_HEADS):
        cols = []
        for g in range(2):
            h = 2 * n + g
            qh = qt[h * HEAD_DIM:(h + 1) * HEAD_DIM]
            cols.append(jnp.concatenate([qh, zero] if n % 2 == 0 else [zero, qh], axis=0))
        w_qk.append(jnp.concatenate(cols, axis=1))

    sub = min(128, kba)

    def scores(kblk, bias, n, i):
        r = slice(i * sub, (i + 1) * sub)
        s = _dot(kblk[r, (n // 2) * LANES:(n // 2 + 1) * LANES], w_qk[n])
        return s + jnp.concatenate([bias[r], bias[r]], axis=1)

    def max_block(j, carry):
        rows = pl.ds(pl.multiple_of(j * kba, kba), kba)
        bias = jnp.where((key_scr[rows, :] >= thr) & causal(j, kba), 0.0, NEG)
        bias_scr[rows, :] = bias
        kblk = kb_ref[rows, :]
        out = []
        for n in range(N_KV_HEADS):
            mx = carry[n]
            for i in range(kba // sub):
                s = scores(kblk, bias, n, i)
                mx = jnp.maximum(mx, jnp.max(s.reshape(sub // 8, 8, 2 * tq), axis=0))
            out.append(mx)
        return tuple(out)
    mx = lax.fori_loop(0, nka, max_block, (jnp.full((8, 2 * tq), NEG, F32),) * N_KV_HEADS)
    m_row = [jnp.max(m, axis=0, keepdims=True) for m in mx]

    acc_scr[...] = jnp.zeros(acc_scr.shape, F32)

    def attend_block(j, carry):
        rows = pl.ds(pl.multiple_of(j * kba, kba), kba)
        bias = bias_scr[rows, :]
        kblk = kb_ref[rows, :]
        out = []
        for n in range(N_KV_HEADS):
            ls = carry[n]
            pieces = []
            for i in range(kba // sub):
                p = jnp.exp(scores(kblk, bias, n, i) - m_row[n])
                ls = ls + jnp.sum(p.reshape(sub // 8, 8, 2 * tq), axis=0)
                pieces.append(p.astype(BF16))
            acc_scr[n] += _dot(vt_ref[n * HEAD_DIM:(n + 1) * HEAD_DIM, rows], jnp.concatenate(pieces, axis=0))
            out.append(ls)
        return tuple(out)
    ls = lax.fori_loop(0, nka, attend_block, (jnp.zeros((8, 2 * tq), F32),) * N_KV_HEADS)

    parts = []
    for n in range(N_KV_HEADS):
        o = acc_scr[n] / jnp.sum(ls[n], axis=0, keepdims=True)
        parts += [o[:, :tq], o[:, tq:]]
    o_ref[...] = jnp.concatenate(parts, axis=0).T


def _upper_tri(n):
    r = np.arange(n)
    return jnp.asarray(r[:, None] <= r[None, :], BF16)


def _lower_tri(n):
    r = np.arange(n)
    return jnp.asarray(r[:, None] >= r[None, :], BF16)


def _dsa_prompt(qt, qit, wit, kb, kib, vt, b, s_len, tq=256, kbs=512, kba=512, kbt=256):
    kbs, kba, kbt = min(kbs, s_len), min(kba, s_len), min(kbt, s_len)
    n_sel = min(TOPK_KEYS, s_len // 4)
    nq = s_len // tq
    qspec = lambda r: pl.BlockSpec((r, tq), lambda bi, qi_: (0, bi * nq + qi_))
    kspec = lambda w: pl.BlockSpec((s_len, w), lambda bi, qi_: (bi, 0))
    return pl.pallas_call(
        functools.partial(_dsa_prompt_kernel, tq=tq, kbs=kbs, kba=kba, n_sel=n_sel),
        grid=(b, nq),
        in_specs=[qspec(512), qspec(256), qspec(8), kspec(256), kspec(IDX_DIM),
                  pl.BlockSpec((256, s_len), lambda bi, qi_: (0, bi)),
                  pl.BlockSpec((kbt, kbt), lambda bi, qi_: (0, 0))],
        out_specs=pl.BlockSpec((tq, D_ATT), lambda bi, qi_: (bi * nq + qi_, 0)),
        out_shape=jax.ShapeDtypeStruct((b * s_len, D_ATT), F32),
        scratch_shapes=[pltpu.VMEM((s_len, tq), F32),
                        pltpu.VMEM((s_len, tq), F32),
                        pltpu.VMEM((N_KV_HEADS, HEAD_DIM, 2 * tq), F32)],
        compiler_params=_cparams("arbitrary", "arbitrary"),
        name="dsa_prompt",
    )(qt, qit, wit, kb, kib, vt, _lower_tri(kbt))


def _dsa_sample_kernel(pt_ref, q_ref, qis_ref, wi_ref, knew_ref, vnew_ref, kinew_ref, tri_ref,
                       ck_ref, cv_ref, cki_ref, o_ref,
                       kibuf, kbuf, vbuf, sc_scr, sem_ki, sem_kv,
                       *, n_pages, n_sel, kbi, kba, ts):
    b = pl.program_id(0)
    nb = pl.num_programs(0)
    slot = b % 2
    past = n_pages * PAGE_SIZE
    rows = N_HEADS * ts
    hrows = rows // N_KV_HEADS

    def page_keys(p):
        return pl.ds(pl.multiple_of(p * PAGE_SIZE, PAGE_SIZE), PAGE_SIZE)

    def ki_copy(bb, sl, p):
        return pltpu.make_async_copy(cki_ref.at[pt_ref[bb, p]], kibuf.at[sl, :, page_keys(p)], sem_ki.at[sl])

    def k_copy(bb, p, n):
        return pltpu.make_async_copy(ck_ref.at[pt_ref[bb, p], n], kbuf.at[n, :, page_keys(p)], sem_kv.at[0])

    def v_copy(bb, p, n):
        return pltpu.make_async_copy(cv_ref.at[pt_ref[bb, p], n], vbuf.at[n, :, page_keys(p)], sem_kv.at[1])

    def for_pages(fn):
        def body(p, _):
            fn(p)
            return 0
        lax.fori_loop(0, n_pages, body, 0)

    def for_page_heads(fn):
        for_pages(lambda p: [fn(p, n) for n in range(N_KV_HEADS)])

    @pl.when(b == 0)
    def _():
        for_pages(lambda p: ki_copy(b, slot, p).start())

    def start_kv(p, n):
        k_copy(b, p, n).start()
        v_copy(b, p, n).start()
    for_page_heads(start_kv)

    @pl.when(b + 1 < nb)
    def _():
        for_pages(lambda p: ki_copy(b + 1, 1 - slot, p).start())

    for_pages(lambda p: ki_copy(b, slot, p).wait())

    qis = qis_ref[...]
    wi = wi_ref[...]
    n_tiles = past // LANES + 1

    def index_scores(dots):
        d = jnp.maximum(dots, 0.0)
        sc = jnp.zeros((ts, dots.shape[1]), F32)
        for h in range(N_IDX_HEADS):
            sc += d[h * ts:(h + 1) * ts] * wi[:, h:h + 1]
        return sc

    def score_block(j, _):
        cols = pl.ds(pl.multiple_of(j * kbi, kbi), kbi)
        sc_scr[:, cols] = index_scores(_dot(qis, kibuf[slot, :, cols].astype(BF16)))
        return 0
    lax.fori_loop(0, past // kbi, score_block, 0)
    pad_rows = LANES - ts
    kin = jnp.concatenate([kinew_ref[...], jnp.zeros((pad_rows, IDX_DIM), F32)], axis=0)
    new_ok = (lax.broadcasted_iota(jnp.int32, (ts, LANES), 1)
              <= lax.broadcasted_iota(jnp.int32, (ts, LANES), 0))
    sc_scr[:, past:past + LANES] = jnp.where(new_ok, index_scores(_dot_nt(qis, kin.astype(BF16))), -jnp.inf)

    def counts(preds):
        sc = sc_scr[...]
        parts = [[jnp.zeros((ts, LANES), F32), jnp.zeros((ts, LANES), F32)] for _ in preds]
        for c in range(n_tiles):
            tile = sc[:, c * LANES:(c + 1) * LANES]
            for i, pred in enumerate(preds):
                parts[i][c % 2] = parts[i][c % 2] + jnp.where(pred(tile), 1.0, 0.0)
        return [jnp.sum(a + b_, axis=-1, keepdims=True) for a, b_ in parts]

    thr = _kth_largest_radix4(lambda c1, c2, c3: counts([lambda s: s >= c1, lambda s: s >= c2, lambda s: s >= c3]),
                              (ts, 1), n_sel)
    n_ge, n_gt = counts([lambda s: s >= thr, lambda s: s > thr])

    @pl.when(jnp.max(n_ge) > n_sel)
    def _():
        room = n_sel - n_gt
        kbt = tri_ref.shape[0]

        def demote(cols, tie_seen, tri):
            sc = sc_scr[:, cols]
            tie = jnp.where(sc == thr, 1.0, 0.0)
            rank = tie_seen + _dot(tie.astype(BF16), tri)
            sc_scr[:, cols] = jnp.where((tie > 0.0) & (rank > room), -jnp.inf, sc)
            return tie_seen + jnp.sum(tie, axis=-1, keepdims=True)
        seen = lax.fori_loop(0, past // kbt,
                             lambda j, s: demote(pl.ds(pl.multiple_of(j * kbt, kbt), kbt), s, tri_ref[...]),
                             jnp.zeros((ts, 1), F32))
        demote(pl.ds(past, LANES), seen, tri_ref[:LANES, :LANES])

    sc_scr[...] = jnp.where(sc_scr[...] >= thr, 0.0, NEG)
    sc_scr[:, past:past + LANES] = jnp.where(new_ok, sc_scr[:, past:past + LANES], NEG)

    for_page_heads(lambda p, n: k_copy(b, p, n).wait())
    for_page_heads(lambda p, n: v_copy(b, p, n).wait())

    q = q_ref[...]
    zpad = jnp.zeros((pad_rows, HEAD_DIM), F32)

    def key_block(j):
        return pl.ds(pl.multiple_of(j * kba, kba), kba)

    def head_rows(x, n):
        return x[n * hrows:(n + 1) * hrows]

    def scores(dots_of_head, bias):
        s = jnp.concatenate([dots_of_head(head_rows(q, n), n) for n in range(N_KV_HEADS)], axis=0)
        return s + jnp.concatenate([bias] * N_HEADS, axis=0)

    def block_scores(j):
        return scores(lambda qn, n: _dot(qn, kbuf[n, :, key_block(j)].astype(BF16)), sc_scr[:, key_block(j)])

    def new_scores():
        return scores(lambda qn, n: _dot_nt(qn, jnp.concatenate([knew_ref[n], zpad], axis=0).astype(BF16)),
                      sc_scr[:, past:past + LANES])

    def lane_tiles(x, op, acc):
        for c in range(x.shape[1] // LANES):
            acc = op(acc, x[:, c * LANES:(c + 1) * LANES])
        return acc

    mx = lax.fori_loop(0, past // kba, lambda j, m: lane_tiles(block_scores(j), jnp.maximum, m),
                       jnp.full((rows, LANES), NEG, F32))
    m_row = jnp.max(jnp.maximum(mx, new_scores()), axis=-1, keepdims=True)

    def value_product(p, pv_of_head, accs):
        return tuple(accs[n] + pv_of_head(head_rows(p, n).astype(BF16), n) for n in range(N_KV_HEADS))

    def attend_block(j, carry):
        ls, accs = carry
        p = jnp.exp(block_scores(j) - m_row)
        accs = value_product(p, lambda pn, n: _dot_nt(pn, vbuf[n, :, key_block(j)].astype(BF16)), accs)
        return lane_tiles(p, jnp.add, ls), accs
    zero_acc = (jnp.zeros((hrows, HEAD_DIM), F32),) * N_KV_HEADS
    ls, accs = lax.fori_loop(0, past // kba, attend_block, (jnp.zeros((rows, LANES), F32), zero_acc))
    p_new = jnp.exp(new_scores() - m_row)
    accs = value_product(
        p_new, lambda pn, n: _dot(pn, jnp.concatenate([vnew_ref[n], zpad], axis=0).astype(BF16)), accs)
    l_row = jnp.sum(ls + p_new, axis=-1, keepdims=True)
    o_ref[...] = jnp.concatenate(accs, axis=0) / l_row


def _dsa_sample(q, qi, wi, k, v, ki, cache_k, cache_v, cache_kidx, page_table, ts, kbi=2048, kba=1024, kbt=512):
    bd, n_pages = page_table.shape
    past = n_pages * PAGE_SIZE
    kbi, kba, kbt = min(kbi, past), min(kba, past), min(kbt, past)
    n_sel = min(TOPK_KEYS, (past + ts) // 4)
    rows = N_HEADS * ts
    heads_first = lambda a, nh: a.reshape(bd, ts, nh, -1).transpose(0, 2, 1, 3)
    qh = heads_first(q, N_HEADS).reshape(bd, rows, HEAD_DIM)
    qis = heads_first(qi, N_IDX_HEADS).reshape(bd, N_IDX_HEADS * ts, IDX_DIM)
    per_b = lambda *s: pl.BlockSpec((None,) + s, lambda i, pt: (i,) + (0,) * len(s))
    any_spec = pl.BlockSpec(memory_space=pl.ANY)
    page_buf = pltpu.VMEM((N_KV_HEADS, HEAD_DIM, past), F32)
    o = pl.pallas_call(
        functools.partial(_dsa_sample_kernel, n_pages=n_pages, n_sel=n_sel, kbi=kbi, kba=kba, ts=ts),
        grid_spec=pltpu.PrefetchScalarGridSpec(
            num_scalar_prefetch=1,
            grid=(bd,),
            in_specs=[per_b(rows, HEAD_DIM), per_b(N_IDX_HEADS * ts, IDX_DIM), per_b(ts, N_IDX_HEADS),
                      per_b(N_KV_HEADS, ts, HEAD_DIM), per_b(N_KV_HEADS, ts, HEAD_DIM), per_b(ts, IDX_DIM),
                      pl.BlockSpec((kbt, kbt), lambda i, pt: (0, 0)),
                      any_spec, any_spec, any_spec],
            out_specs=per_b(rows, HEAD_DIM),
            scratch_shapes=[pltpu.VMEM((2, IDX_DIM, past), F32), page_buf, page_buf,
                            pltpu.VMEM((ts, past + LANES), F32),
                            pltpu.SemaphoreType.DMA((2,)),
                            pltpu.SemaphoreType.DMA((2,))]),
        out_shape=jax.ShapeDtypeStruct((bd, rows, HEAD_DIM), F32),
        compiler_params=_cparams("arbitrary"),
        name="dsa_sample",
    )(page_table, qh, qis, wi.reshape(bd, ts, N_IDX_HEADS), heads_first(k, N_KV_HEADS),
      heads_first(v, N_KV_HEADS), ki.reshape(bd, ts, IDX_DIM), _upper_tri(kbt),
      cache_k.transpose(0, 2, 3, 1), cache_v.transpose(0, 2, 3, 1), cache_kidx.transpose(0, 2, 1))
    return o.reshape(bd, N_HEADS, ts, HEAD_DIM).transpose(0, 2, 1, 3).reshape(bd * ts, D_ATT)


def _layer_norm(x, g, b):
    mu = jnp.mean(x, axis=-1, keepdims=True)
    xc = x - mu
    var = jnp.mean(xc * xc, axis=-1, keepdims=True)
    return xc * lax.rsqrt(var + LN_EPS) * g + b


def _rms_norm(x, g):
    return x * lax.rsqrt(jnp.mean(x * x, axis=-1, keepdims=True) + LN_EPS) * g


def _mix_kernel(*refs):
    ys_refs, refs = refs[:S5_BLOCKS], refs[S5_BLOCKS:]
    (ya_ref, x_ref, g1_ref, sc2_ref, sh2_ref, wglu_ref, bglu_ref, gs_ref, ga_ref,
     wout_ref, l1g_ref, l1b_ref, wr_ref, br_ref, tri_ref,
     x1_ref, h2_ref, eidx_ref, gate_ref, rank_ref, cnt_ref, run_scr) = refs
    g = jax.nn.gelu(jnp.concatenate([r[...] for r in ys_refs], axis=-1))
    ssm = g * jax.nn.sigmoid(_dot(g.astype(BF16), wglu_ref[...]) + bglu_ref[...])
    cat = jnp.concatenate([_rms_norm(ssm, gs_ref[...]), _rms_norm(ya_ref[...], ga_ref[...])], axis=-1)
    mixed = _dot(cat.astype(BF16), wout_ref[...])
    x1 = _layer_norm(ALPHA * x_ref[...] + g1_ref[...] * mixed, l1g_ref[...], l1b_ref[...])
    x1_ref[...] = x1
    h2 = (x1 * (1.0 + sc2_ref[...]) + sh2_ref[...]).astype(BF16)
    h2_ref[...] = h2
    aff = jax.nn.sigmoid(_dot(h2, wr_ref[...]))
    vals = aff + br_ref[...]
    tm = aff.shape[0]
    lane = lax.broadcasted_iota(jnp.int32, (tm, N_EXPERTS), 1)
    col = lax.broadcasted_iota(jnp.int32, (tm, 8), 1)
    eidx = jnp.zeros((tm, 8), jnp.int32)
    gsel = jnp.zeros((tm, 8), F32)
    chosen = jnp.zeros((tm, N_EXPERTS), F32)
    picks = []
    for j in range(TOP_K_EXPERTS):
        top = jnp.max(vals, axis=-1, keepdims=True)
        idx = jnp.min(jnp.where(vals == top, lane, N_EXPERTS), axis=-1, keepdims=True)
        hit = lane == idx
        gj = jnp.sum(jnp.where(hit, aff, 0.0), axis=-1, keepdims=True)
        eidx = jnp.where(col == j, idx, eidx)
        gsel = jnp.where(col == j, gj, gsel)
        vals = jnp.where(hit, -jnp.inf, vals)
        chosen = jnp.where(hit, 1.0, chosen)
        picks.append(idx)
    eidx_ref[...] = eidx
    gate_ref[...] = gsel / jnp.sum(gsel, axis=-1, keepdims=True) * ROUTED_SCALE

    @pl.when(pl.program_id(0) == 0)
    def _():
        run_scr[...] = jnp.zeros(run_scr.shape, F32)
    before = run_scr[...] + _dot(tri_ref[...], chosen.astype(BF16))
    rank = jnp.zeros((tm, 8), jnp.int32)
    for j in range(TOP_K_EXPERTS):
        rj = jnp.sum(jnp.where(lane == picks[j], before, 0.0), axis=-1, keepdims=True)
        rank = jnp.where(col == j, rj.astype(jnp.int32), rank)
    rank_ref[...] = rank
    run_scr[...] += jnp.sum(chosen, axis=0, keepdims=True)
    cnt_ref[...] = jnp.broadcast_to(run_scr[...], cnt_ref.shape)


def _mixer_out(ys, ya, x, g1, sc2, sh2, w, rows_per_mod, tm):
    t = x.shape[0]
    ms = _mod_spec(g1, tm, rows_per_mod)
    row = lambda wd: pl.BlockSpec((tm, wd), lambda i: (i, 0))
    full = lambda a: pl.BlockSpec(a.shape, lambda i: (0, 0))
    r = np.arange(tm)
    strict_lower = jnp.asarray(r[None, :] < r[:, None], BF16)
    consts = [w['w_glu'], w['b_glu'], w['g_ssm_out'], w['g_att_out'], w['w_out'], w['ln1_g'], w['ln1_b'],
              w['w_router'], w['b_router'], strict_lower]
    return pl.pallas_call(
        _mix_kernel,
        grid=(t // tm,),
        in_specs=[row(LANES)] * S5_BLOCKS + [row(D_ATT), row(D_MODEL), ms, ms, ms]
                 + [full(a) for a in consts],
        out_specs=[row(D_MODEL), row(D_MODEL), row(8), row(8), row(8),
                   pl.BlockSpec((8, N_EXPERTS), lambda i: (0, 0))],
        out_shape=[jax.ShapeDtypeStruct((t, D_MODEL), F32), jax.ShapeDtypeStruct((t, D_MODEL), BF16),
                   jax.ShapeDtypeStruct((t, 8), jnp.int32), jax.ShapeDtypeStruct((t, 8), F32),
                   jax.ShapeDtypeStruct((t, 8), jnp.int32), jax.ShapeDtypeStruct((8, N_EXPERTS), F32)],
        scratch_shapes=[pltpu.VMEM((1, N_EXPERTS), F32)],
        compiler_params=_cparams("arbitrary"),
        name="mixer_out",
    )(*ys, ya, x, g1, sc2, sh2, *consts)


MOE_ROWS = 256


def _experts_kernel(be_ref, nused_ref, x_ref, w1_ref, w3_ref, w2_ref, y_ref):
    i = pl.program_id(0)

    @pl.when(i < nused_ref[0])
    def _():
        x = x_ref[...]
        h1 = _dot(x, w1_ref[...].astype(BF16))
        h3 = _dot(x, w3_ref[...].astype(BF16))
        a = (h1 * jax.nn.sigmoid(h1) * h3).astype(BF16)
        y_ref[...] = _dot(a, w2_ref[...].astype(BF16))

    @pl.when(i >= nused_ref[0])
    def _():
        y_ref[...] = jnp.zeros(y_ref.shape, F32)


def _experts(xs, block_e, n_used, w_e1, w_e3, w_e2):
    n_slots = xs.shape[0]
    n_blocks = n_slots // MOE_ROWS
    wspec = lambda r, c: pl.BlockSpec((None, r, c), lambda i, be, nu: (be[i], 0, 0))
    return pl.pallas_call(
        _experts_kernel,
        grid_spec=pltpu.PrefetchScalarGridSpec(
            num_scalar_prefetch=2,
            grid=(n_blocks,),
            in_specs=[pl.BlockSpec((MOE_ROWS, D_MODEL), lambda i, be, nu: (i, 0)),
                      wspec(D_MODEL, D_EXPERT), wspec(D_MODEL, D_EXPERT), wspec(D_EXPERT, D_MODEL)],
            out_specs=pl.BlockSpec((MOE_ROWS, D_MODEL), lambda i, be, nu: (i, 0))),
        out_shape=jax.ShapeDtypeStruct((n_slots, D_MODEL), F32),
        compiler_params=_cparams("arbitrary"),
        name="routed_experts",
    )(block_e, n_used, xs, w_e1, w_e3, w_e2)


def _dispatch_plan(eidx, rank, counts, n_tok):
    padded = (counts + MOE_ROWS - 1) // MOE_ROWS * MOE_ROWS
    pends = jnp.cumsum(padded)
    pstarts = pends - padded
    slot = pstarts[eidx] + rank
    n_blocks = -(-n_tok * TOP_K_EXPERTS // MOE_ROWS) + N_EXPERTS
    first_row = jnp.arange(n_blocks, dtype=jnp.int32) * MOE_ROWS
    block_e = jnp.sum((pends[None, :] <= first_row[:, None]).astype(jnp.int32), axis=1)
    block_e = jnp.minimum(block_e, N_EXPERTS - 1)
    n_used = (pends[-1:] // MOE_ROWS).astype(jnp.int32)
    return slot, block_e, n_used, n_blocks * MOE_ROWS


def _final_kernel(*refs):
    ye_refs = refs[:TOP_K_EXPERTS]
    gate_ref, h2_ref, x1_ref, g2_ref, ws1_ref, ws3_ref, ws2_ref, l2g_ref, l2b_ref, o_ref = refs[TOP_K_EXPERTS:]
    h2 = h2_ref[...]
    h1 = _dot(h2, ws1_ref[...])
    h3 = _dot(h2, ws3_ref[...])
    a = (h1 * jax.nn.sigmoid(h1) * h3).astype(BF16)
    gate = gate_ref[...]
    routed = ye_refs[0][...] * gate[:, 0:1]
    for j in range(1, TOP_K_EXPERTS):
        routed += ye_refs[j][...] * gate[:, j:j + 1]
    ffn = routed + _dot(a, ws2_ref[...])
    o_ref[...] = _layer_norm(ALPHA * x1_ref[...] + g2_ref[...] * ffn, l2g_ref[...], l2b_ref[...])


def _final(ye_parts, gates, h2, x1, g2, w, rows_per_mod, tm):
    t = x1.shape[0]
    row = lambda wd: pl.BlockSpec((tm, wd), lambda i: (i, 0))
    full = lambda a: pl.BlockSpec(a.shape, lambda i: (0, 0))
    consts = [w['w_s1'], w['w_s3'], w['w_s2'], w['ln2_g'], w['ln2_b']]
    return pl.pallas_call(
        _final_kernel,
        grid=(t // tm,),
        in_specs=[row(D_MODEL)] * TOP_K_EXPERTS
                 + [row(8), row(D_MODEL), row(D_MODEL), _mod_spec(g2, tm, rows_per_mod)]
                 + [full(a) for a in consts],
        out_specs=row(D_MODEL),
        out_shape=jax.ShapeDtypeStruct((t, D_MODEL), F32),
        compiler_params=_cparams("arbitrary"),
        name="final_residual",
    )(*ye_parts, gates, h2, x1, g2, *consts)


def kernel(x_prompt, x_sample, c_prompt, c_sample, cache_k, cache_v, cache_kidx, state_ssm_re, state_ssm_im, page_table, w_ada, b_ada, w_in, ssm_a_re, ssm_a_im, ssm_log_dt, ssm_b_re, ssm_b_im, ssm_c_re, ssm_c_im, ssm_d, w_glu, b_glu, g_ssm_out, g_att_out, w_out, ln1_g, ln1_b, w_router, b_router, w_e1, w_e3, w_e2, w_s1, w_s3, w_s2, ln2_g, ln2_b):
    assert w_in.shape[0] == 1, "one layer"
    b, s, d = x_prompt.shape
    bd, ts, _ = x_sample.shape
    tp, tsn = b * s, bd * ts
    tm = 512
    row = lambda a: a[0].reshape(1, -1)
    w = dict(w_glu=w_glu[0].astype(BF16), b_glu=row(b_glu), g_ssm_out=row(g_ssm_out), g_att_out=row(g_att_out),
             w_out=w_out[0].astype(BF16), ln1_g=row(ln1_g), ln1_b=row(ln1_b),
             w_router=w_router[0].astype(BF16), b_router=row(b_router),
             w_s1=w_s1[0].astype(BF16), w_s3=w_s3[0].astype(BF16), w_s2=w_s2[0].astype(BF16),
             ln2_g=row(ln2_g), ln2_b=row(ln2_b))
    w_sample, w_prompt, w_prompt_t = _projection_weights(w_in[0])
    ssm_args = (ssm_a_re[0], ssm_a_im[0], ssm_log_dt[0], ssm_b_re[0], ssm_b_im[0], ssm_c_re[0], ssm_c_im[0],
                ssm_d[0])

    mod = _ada_mod(jnp.concatenate([c_prompt, c_sample], axis=0), w_ada[0], b_ada[0])
    mod_p = mod[:b].reshape(b, 6, 1, d)
    mod_s = jnp.broadcast_to(mod[b:].reshape(bd, 1, 6, d), (bd, ts, 6, d)).reshape(tsn, 6, d)

    xp = x_prompt.reshape(tp, d)
    u, k_p, v_p, ki_p, kb, kib, qt, qit, vt, wit = _in_projection_prompt(
        xp, mod_p[:, 1], mod_p[:, 0], w_prompt, w_prompt_t, s, tm)
    mats = _ssm_matrices(*ssm_args)
    zero_state = jnp.zeros((b, N_SSM_GROUPS, SSM_STATE), F32)
    ys_p, re_p, im_p = _s5_scan(u.reshape(b, s, D_SSM), zero_state, zero_state, mats, True, min(s, 128))
    ya_p = _dsa_prompt(qt, qit, wit, kb, kib, vt, b, s)
    x1_p, h2_p, eidx_p, gate_p, rank_p, cnt_p = _mixer_out([y.reshape(tp, LANES) for y in ys_p], ya_p, xp,
                                                           mod_p[:, 2], mod_p[:, 4], mod_p[:, 3], w, s, tm)

    assert ts == S5_CHUNK, "each sample sequence is one S5 chunk"
    xs = x_sample.reshape(tsn, d)
    tms = min(tm, tsn)
    u, q, k_s, v_s, qi, ki_s, wi = _in_projection(xs, mod_s[:, 1], mod_s[:, 0], w_sample, None, tms)
    ys_s, re_s, im_s = _s5_scan(u.reshape(1, tsn, D_SSM), state_ssm_re[0], state_ssm_im[0], mats, False, tsn)
    ya_s = _dsa_sample(q, qi, wi, k_s, v_s, ki_s, cache_k[0], cache_v[0], cache_kidx[0], page_table, ts)
    x1_s, h2_s, eidx_s, gate_s, rank_s, cnt_s = _mixer_out([y.reshape(tsn, LANES) for y in ys_s], ya_s, xs,
                                                           mod_s[:, 2], mod_s[:, 4], mod_s[:, 3], w, None, tms)

    n_tok = tp + tsn
    cnt_p, cnt_s = cnt_p[0].astype(jnp.int32), cnt_s[0].astype(jnp.int32)
    eidx_p, eidx_s = eidx_p[:, :TOP_K_EXPERTS], eidx_s[:, :TOP_K_EXPERTS]
    eidx = jnp.concatenate([eidx_p, eidx_s], axis=0)
    rank = jnp.concatenate([rank_p[:, :TOP_K_EXPERTS], rank_s[:, :TOP_K_EXPERTS] + cnt_p[eidx_s]], axis=0)
    slot, block_e, n_used, n_slots = _dispatch_plan(eidx, rank, cnt_p + cnt_s, n_tok)
    tok = jnp.broadcast_to(jnp.arange(n_tok, dtype=jnp.int32)[:, None], slot.shape)
    slot_tok = jnp.zeros((n_slots,), jnp.int32).at[slot.reshape(-1)].set(tok.reshape(-1))
    h2 = jnp.concatenate([h2_p, h2_s], axis=0)
    ye = _experts(h2[slot_tok], block_e, n_used, w_e1[0], w_e3[0], w_e2[0])
    parts_p = [ye[slot[:tp, j]] for j in range(TOP_K_EXPERTS)]
    parts_s = [ye[slot[tp:, j]] for j in range(TOP_K_EXPERTS)]

    y_p = _final(parts_p, gate_p, h2_p, x1_p, mod_p[:, 5], w, s, tm)
    y_s = _final(parts_s, gate_s, h2_s, x1_s, mod_s[:, 5], w, None, tms)

    kvs = (1, b, s, N_KV_HEADS, HEAD_DIM)
    kvd = (1, bd, ts, N_KV_HEADS, HEAD_DIM)
    return (y_p.reshape(b, s, d), y_s.reshape(bd, ts, d),
            k_p.reshape(kvs), v_p.reshape(kvs), ki_p.reshape(1, b, s, IDX_DIM), re_p[None], im_p[None],
            k_s.reshape(kvd), v_s.reshape(kvd), ki_s.reshape(1, bd, ts, IDX_DIM), re_s[None], im_s[None])
```

```python
import functools
import math

import jax
import jax.numpy as jnp
import numpy as np
from jax import lax
from jax.experimental import pallas as pl
from jax.experimental.pallas import tpu as pltpu

D_MODEL = 1024
D_SSM = 512
SSM_GROUP = 16
N_SSM_GROUPS = 32
SSM_STATE = 64
D_ATT = 512
HEAD_DIM = 64
N_HEADS = 8
N_KV_HEADS = 4
N_IDX_HEADS = 4
IDX_DIM = 64
TOPK_KEYS = 256
N_EXPERTS = 64
TOP_K_EXPERTS = 6
D_EXPERT = 256
ROUTED_SCALE = 2.5
PAGE_SIZE = 128
ALPHA = 2.0 ** 0.25
LN_EPS = 1e-5

LANES = 128
VMEM_LIMIT = 56 * 1024 * 1024
NEG = -1e30
INT_MIN = -2 ** 31

F32 = jnp.float32
BF16 = jnp.bfloat16
HI = lax.Precision.HIGHEST


def _cparams(*sem):
    return pltpu.CompilerParams(dimension_semantics=sem, vmem_limit_bytes=VMEM_LIMIT)


def _dot(a, b):
    return jnp.dot(a, b, preferred_element_type=F32)


def _dot_nt(a, b):
    return lax.dot_general(a, b, (((1,), (1,)), ((), ())), preferred_element_type=F32)


def _ada_kernel(c_ref, w_ref, b_ref, o_ref):
    c = c_ref[...]
    s = (c * jax.nn.sigmoid(c)).astype(BF16)
    o_ref[...] = _dot(s, w_ref[...].astype(BF16)) + b_ref[...]


def _ada_mod(c, w_ada, b_ada):
    n, d = c.shape
    dn = w_ada.shape[1]
    tn = 1024
    return pl.pallas_call(
        _ada_kernel,
        grid=(dn // tn,),
        in_specs=[pl.BlockSpec((n, d), lambda j: (0, 0)),
                  pl.BlockSpec((d, tn), lambda j: (0, j)),
                  pl.BlockSpec((1, tn), lambda j: (0, j))],
        out_specs=pl.BlockSpec((n, tn), lambda j: (0, j)),
        out_shape=jax.ShapeDtypeStruct((n, dn), F32),
        compiler_params=_cparams("arbitrary"),
        name="ada_mod",
    )(c, w_ada, b_ada.reshape(1, dn))


_C_U, _C_Q, _C_K, _C_V, _C_QI, _C_KI, _C_WI, _C_END = 0, 512, 1024, 1280, 1536, 1792, 1920, 2048
_P_U, _P_K, _P_V, _P_KI, _P_END = 0, 512, 768, 1024, 1152
_T_Q, _T_QI, _T_V, _T_WI, _T_END = 0, 512, 768, 1024, 1032


def _projection_weights(w_in):
    d = w_in.shape[0]
    cuts = np.cumsum([D_SSM, N_HEADS * HEAD_DIM, N_KV_HEADS * HEAD_DIM, N_KV_HEADS * HEAD_DIM,
                      N_IDX_HEADS * IDX_DIM, IDX_DIM])
    wu, wq, wk, wv, wqi, wki, wwi = jnp.split(w_in, cuts.tolist(), axis=1)
    wq, wqi, wwi = wq * HEAD_DIM ** -0.5, wqi * IDX_DIM ** -0.5, wwi * N_IDX_HEADS ** -0.5
    z = lambda n: jnp.zeros((d, n), w_in.dtype)
    sample = jnp.concatenate([wu, wq, wk, wv, wqi, wki, z(64), wwi, z(124)], axis=1)
    prompt = jnp.concatenate([wu, wk, wv, wki, z(64)], axis=1)
    prompt_t = jnp.concatenate([wq, wqi, wv, wwi, z(4)], axis=1).T
    return sample.astype(BF16), prompt.astype(BF16), prompt_t.astype(BF16)


def _inproj_kernel(x_ref, sc_ref, sh_ref, w_ref, u_ref, q_ref, k_ref, v_ref, qi_ref, ki_ref, wi_ref):
    h = (x_ref[...] * (1.0 + sc_ref[...]) + sh_ref[...]).astype(BF16)
    r = _dot(h, w_ref[...])
    u_ref[...] = r[:, _C_U:_C_Q]
    q_ref[...] = r[:, _C_Q:_C_K].astype(BF16)
    k_ref[...] = r[:, _C_K:_C_V]
    v_ref[...] = r[:, _C_V:_C_QI]
    qi_ref[...] = r[:, _C_QI:_C_KI].astype(BF16)
    ki_ref[...] = r[:, _C_KI:_C_KI + IDX_DIM]
    wi_ref[...] = r[:, _C_WI:_C_WI + N_IDX_HEADS]


def _inproj_prompt_kernel(x_ref, sc_ref, sh_ref, w_ref, wt_ref,
                          u_ref, k_ref, v_ref, ki_ref, kb_ref, kib_ref, qt_ref, qit_ref, vt_ref, wit_ref):
    h = (x_ref[...] * (1.0 + sc_ref[...]) + sh_ref[...]).astype(BF16)
    r = _dot(h, w_ref[...])
    u_ref[...] = r[:, _P_U:_P_K]
    k = r[:, _P_K:_P_V]
    k_ref[...] = k
    kb_ref[...] = k.astype(BF16)
    v_ref[...] = r[:, _P_V:_P_KI]
    ki = r[:, _P_KI:_P_KI + IDX_DIM]
    ki_ref[...] = ki
    kib_ref[...] = ki.astype(BF16)
    rt = _dot_nt(wt_ref[...], h)
    qt_ref[...] = rt[_T_Q:_T_QI].astype(BF16)
    qit_ref[...] = rt[_T_QI:_T_V].astype(BF16)
    vt_ref[...] = rt[_T_V:_T_WI].astype(BF16)
    wit_ref[...] = rt[_T_WI:_T_END]


def _mod_spec(mod, tm, rows_per_mod):
    if rows_per_mod is None:
        return pl.BlockSpec((tm, D_MODEL), lambda i: (i, 0))
    tiles = rows_per_mod // tm
    return pl.BlockSpec((None, 1, D_MODEL), lambda i: (i // tiles, 0, 0))


def _in_projection(x, sc, sh, w_tok, rows_per_mod, tm):
    t = x.shape[0]
    widths = [(D_SSM, F32), (512, BF16), (256, F32), (256, F32), (256, BF16), (IDX_DIM, F32),
              (N_IDX_HEADS, F32)]
    ms = _mod_spec(sc, tm, rows_per_mod)
    return pl.pallas_call(
        _inproj_kernel,
        grid=(t // tm,),
        in_specs=[pl.BlockSpec((tm, D_MODEL), lambda i: (i, 0)), ms, ms,
                  pl.BlockSpec(w_tok.shape, lambda i: (0, 0))],
        out_specs=[pl.BlockSpec((tm, w), lambda i: (i, 0)) for w, _ in widths],
        out_shape=[jax.ShapeDtypeStruct((t, w), dt) for w, dt in widths],
        compiler_params=_cparams("arbitrary"),
        name="in_projection",
    )(x, sc, sh, w_tok)


def _in_projection_prompt(x, sc, sh, w_tok, w_feat, rows_per_mod, tm):
    t = x.shape[0]
    tok = [(D_SSM, F32), (256, F32), (256, F32), (IDX_DIM, F32), (256, BF16), (IDX_DIM, BF16)]
    feat = [(512, BF16), (256, BF16), (256, BF16), (8, F32)]
    ms = _mod_spec(sc, tm, rows_per_mod)
    return pl.pallas_call(
        _inproj_prompt_kernel,
        grid=(t // tm,),
        in_specs=[pl.BlockSpec((tm, D_MODEL), lambda i: (i, 0)), ms, ms,
                  pl.BlockSpec(w_tok.shape, lambda i: (0, 0)), pl.BlockSpec(w_feat.shape, lambda i: (0, 0))],
        out_specs=[pl.BlockSpec((tm, w), lambda i: (i, 0)) for w, _ in tok]
                  + [pl.BlockSpec((r, tm), lambda i: (0, i)) for r, _ in feat],
        out_shape=[jax.ShapeDtypeStruct((t, w), dt) for w, dt in tok]
                  + [jax.ShapeDtypeStruct((r, t), dt) for r, dt in feat],
        compiler_params=_cparams("arbitrary"),
        name="in_projection_prompt",
    )(x, sc, sh, w_tok, w_feat)


S5_CHUNK = 8
S5_BLOCKS = D_SSM // LANES
S5_GROUPS_PER_BLOCK = LANES // SSM_GROUP
S5_PAIRS = N_SSM_GROUPS // 2


def _ssm_matrices(a_re, a_im, log_dt, b_re, b_im, c_re, c_im, d_skip):
    g, p = a_re.shape
    L = S5_CHUNK
    dt = jnp.exp(log_dt)[:, None]
    mag = jnp.exp(dt * a_re)
    abr, abi = mag * jnp.cos(dt * a_im), mag * jnp.sin(dt * a_im)
    den = a_re * a_re + a_im * a_im
    nr, ni = abr - 1.0, abi
    fr = (nr * a_re + ni * a_im) / den
    fi = (ni * a_re - nr * a_im) / den
    bbr = fr[..., None] * b_re - fi[..., None] * b_im
    bbi = fr[..., None] * b_im + fi[..., None] * b_re
    dd = jnp.arange(L + 1, dtype=F32)[:, None, None]
    pm = jnp.exp(dd * (dt * a_re)[None])
    pw_r = pm * jnp.cos(dd * (dt * a_im)[None])
    pw_i = pm * jnp.sin(dd * (dt * a_im)[None])
    ca_r = c_re[None] * pw_r[:, :, None, :] - c_im[None] * pw_i[:, :, None, :]
    ca_i = c_re[None] * pw_i[:, :, None, :] + c_im[None] * pw_r[:, :, None, :]
    kd = (jnp.einsum('dgop,gpi->dgoi', ca_r[:L], bbr, precision=HI)
          - jnp.einsum('dgop,gpi->dgoi', ca_i[:L], bbi, precision=HI))
    nblk, gblk, npair = S5_BLOCKS, S5_GROUPS_PER_BLOCK, S5_PAIRS
    ppb = npair // nblk
    kd_t = jnp.transpose(kd, (1, 0, 3, 2)).reshape(nblk, gblk, L, SSM_GROUP, SSM_GROUP)
    mtd = jnp.einsum('qgdio,gh->qdgiho', kd_t, jnp.eye(gblk, dtype=F32))
    mtd = mtd.reshape(nblk, L, LANES, LANES)
    sel = np.zeros((ppb, gblk, 2), np.float32)
    for r in range(ppb):
        for t in range(2):
            sel[r, 2 * r + t, t] = 1.0
    sel = jnp.asarray(sel)
    rev_r, rev_i = pw_r[:L][::-1], pw_i[:L][::-1]
    f_r = rev_r[..., None] * bbr[None] - rev_i[..., None] * bbi[None]
    f_i = rev_r[..., None] * bbi[None] + rev_i[..., None] * bbr[None]

    def place_in(f):
        f = f.reshape(L, nblk, ppb, 2, p, SSM_GROUP)
        f = jnp.einsum('mqrtPi,rgt->qrmgitP', f, sel)
        return f.reshape(npair, L, LANES, 2 * p).astype(BF16)

    def place_out(c):
        c = c.reshape(L, nblk, ppb, 2, SSM_GROUP, p)
        c = jnp.einsum('jqrtoP,rgt->qrjtPgo', c, sel)
        return c.reshape(npair, L, 2 * p, LANES).astype(BF16)

    return dict(mtd=mtd.astype(BF16), mf_r=place_in(f_r), mf_i=place_in(f_i),
                mg_r=place_out(ca_r[1:]), mg_i=place_out(-ca_i[1:]),
                al_r=pw_r[L].reshape(npair, 1, 2 * p), al_i=pw_i[L].reshape(npair, 1, 2 * p),
                dsk=d_skip.reshape(nblk, 1, LANES))


def _s5_kernel(*refs, nseq, nc, chained):
    u_refs, refs = refs[:S5_BLOCKS], refs[S5_BLOCKS:]
    (s0r_ref, s0i_ref, mtd_ref, mfr_ref, mfi_ref, mgr_ref, mgi_ref, alr_ref, ali_ref, dsk_ref), refs = \
        refs[:10], refs[10:]
    y_refs, refs = refs[:S5_BLOCKS], refs[S5_BLOCKS:]
    sr_ref, si_ref, xb_scr, inr_scr, ini_scr, str_scr, sti_scr, sbr_scr, sbi_scr, car_r, car_i = refs
    L = S5_CHUNK
    rows = nseq * nc
    ppb = S5_PAIRS // S5_BLOCKS

    def x_block(m, q):
        return u_refs[q][:, pl.ds(m, nc, stride=L), :].reshape(rows, LANES)

    for m in range(L):
        for q in range(S5_BLOCKS):
            xb_scr[m, q] = x_block(m, q).astype(BF16)

    for p in range(S5_PAIRS):
        q = p // ppb
        acc_r = _dot(xb_scr[0, q], mfr_ref[p, 0])
        acc_i = _dot(xb_scr[0, q], mfi_ref[p, 0])
        for m in range(1, L):
            acc_r += _dot(xb_scr[m, q], mfr_ref[p, m])
            acc_i += _dot(xb_scr[m, q], mfi_ref[p, m])
        inr_scr[p] = acc_r
        ini_scr[p] = acc_i

    if chained:
        @pl.when(pl.program_id(0) == 0)
        def _():
            car_r[...] = s0r_ref[...]
            car_i[...] = s0i_ref[...]

        def step(c, carry):
            idx = pl.ds(c, nseq, stride=nc)
            out = []
            for p in range(S5_PAIRS):
                sr, si = carry[2 * p], carry[2 * p + 1]
                str_scr[p, idx, :] = sr
                sti_scr[p, idx, :] = si
                alr, ali = alr_ref[p], ali_ref[p]
                out.append(alr * sr - ali * si + inr_scr[p, idx, :])
                out.append(alr * si + ali * sr + ini_scr[p, idx, :])
            return tuple(out)

        init = []
        for p in range(S5_PAIRS):
            init += [car_r[p], car_i[p]]
        end = lax.fori_loop(0, nc, step, tuple(init))
        for p in range(S5_PAIRS):
            car_r[p] = end[2 * p]
            car_i[p] = end[2 * p + 1]
        sr_ref[...] = car_r[...]
        si_ref[...] = car_i[...]
    else:
        for p in range(S5_PAIRS):
            sr, si = s0r_ref[p], s0i_ref[p]
            str_scr[p] = sr
            sti_scr[p] = si
            alr, ali = alr_ref[p], ali_ref[p]
            sr_ref[p] = alr * sr - ali * si + inr_scr[p]
            si_ref[p] = alr * si + ali * sr + ini_scr[p]

    for p in range(S5_PAIRS):
        sbr_scr[p] = str_scr[p].astype(BF16)
        sbi_scr[p] = sti_scr[p].astype(BF16)

    for q in range(S5_BLOCKS):
        for j in range(L):
            acc = _dot(xb_scr[j, q], mtd_ref[q, 0])
            for m in range(j):
                acc += _dot(xb_scr[m, q], mtd_ref[q, j - m])
            for r in range(ppb):
                p = q * ppb + r
                acc += _dot(sbr_scr[p], mgr_ref[p, j])
                acc += _dot(sbi_scr[p], mgi_ref[p, j])
            y = acc + dsk_ref[q] * x_block(j, q)
            y_refs[q][:, pl.ds(j, nc, stride=L), :] = y.reshape(nseq, nc, LANES)


def _s5_scan(u, s0_re, s0_im, mats, chained, tt):
    nseq, t, _ = u.shape
    n_states = s0_re.shape[0]
    L = S5_CHUNK
    nc = tt // L
    rows = nseq * nc
    assert rows == n_states or chained
    pairs = lambda s: s.reshape(n_states, S5_PAIRS, 2 * SSM_STATE).transpose(1, 0, 2)
    full = lambda a: pl.BlockSpec(a.shape, lambda i: (0,) * a.ndim)
    consts = [mats[k] for k in ('mtd', 'mf_r', 'mf_i', 'mg_r', 'mg_i', 'al_r', 'al_i', 'dsk')]
    state_shape = jax.ShapeDtypeStruct((S5_PAIRS, n_states, 2 * SSM_STATE), F32)
    state_spec = pl.BlockSpec((S5_PAIRS, n_states, 2 * SSM_STATE), lambda i: (0, 0, 0))
    pair_rows = lambda dt: pltpu.VMEM((S5_PAIRS, rows, LANES), dt)
    lane_block = lambda q: pl.BlockSpec((nseq, tt, LANES), lambda i: (0, i, q))
    out_block = pl.BlockSpec((nseq, tt, LANES), lambda i: (0, i, 0))
    *y, sr, si = pl.pallas_call(
        functools.partial(_s5_kernel, nseq=nseq, nc=nc, chained=chained),
        grid=(t // tt,),
        in_specs=[lane_block(q) for q in range(S5_BLOCKS)] + [state_spec, state_spec]
                 + [full(a) for a in consts],
        out_specs=[out_block] * S5_BLOCKS + [state_spec, state_spec],
        out_shape=[jax.ShapeDtypeStruct((nseq, t, LANES), F32)] * S5_BLOCKS + [state_shape, state_shape],
        scratch_shapes=[pltpu.VMEM((L, S5_BLOCKS, rows, LANES), BF16),
                        pair_rows(F32), pair_rows(F32), pair_rows(F32), pair_rows(F32),
                        pair_rows(BF16), pair_rows(BF16),
                        pltpu.VMEM((S5_PAIRS, n_states, 2 * SSM_STATE), F32),
                        pltpu.VMEM((S5_PAIRS, n_states, 2 * SSM_STATE), F32)],
        compiler_params=_cparams("arbitrary"),
        name="s5_scan",
    )(*([u] * S5_BLOCKS), pairs(s0_re), pairs(s0_im), *consts)
    unpair = lambda s: s.transpose(1, 0, 2).reshape(n_states, N_SSM_GROUPS, SSM_STATE)
    return y, unpair(sr), unpair(si)


KEY_NEG_INF = INT_MIN + 0x7FFFFF


def _key_to_float(x):
    return pltpu.bitcast(jnp.where(x < 0, x ^ jnp.int32(0x7FFFFFFF), x), F32)


def _threshold_of(x):
    return jnp.where(x <= KEY_NEG_INF, -jnp.inf, _key_to_float(x))


def _kth_largest(count_ge, shape, n_sel):
    def bit_step(i, x):
        cand = x + lax.shift_left(jnp.int32(1), 31 - i)
        return jnp.where(count_ge(_key_to_float(cand)) >= n_sel, cand, x)
    return _threshold_of(lax.fori_loop(0, 32, bit_step, jnp.full(shape, INT_MIN, jnp.int32)))


def _kth_largest_radix4(count_ge3, shape, n_sel):
    def step(i, x):
        one = lax.shift_left(jnp.int32(1), 30 - 2 * i)
        c1, c2, c3 = x + one, x + 2 * one, x + 3 * one
        n1, n2, n3 = count_ge3(_key_to_float(c1), _key_to_float(c2), _key_to_float(c3))
        return jnp.where(n3 >= n_sel, c3, jnp.where(n2 >= n_sel, c2, jnp.where(n1 >= n_sel, c1, x)))
    return _threshold_of(lax.fori_loop(0, 16, step, jnp.full(shape, INT_MIN, jnp.int32)))


def _rows_sum(x):
    r, c = x.shape
    return jnp.sum(jnp.sum(x.reshape(r // 8, 8, c), axis=0), axis=0, keepdims=True)


def _rows_max(x):
    r, c = x.shape
    return jnp.max(jnp.max(x.reshape(r // 8, 8, c), axis=0), axis=0, keepdims=True)


def _dsa_prompt_kernel(qt_ref, qit_ref, wit_ref, kb_ref, kib_ref, vt_ref, tri_ref, o_ref,
                       key_scr, bias_scr, acc_scr, *, tq, kbs, kba, n_sel):
    qtile = pl.program_id(1)
    n_keys = qtile * tq + tq
    nks = (n_keys + kbs - 1) // kbs
    nka = (n_keys + kba - 1) // kba

    def causal(j, kb):
        kpos = j * kb + lax.broadcasted_iota(jnp.int32, (kb, tq), 0)
        tpos = qtile * tq + lax.broadcasted_iota(jnp.int32, (kb, tq), 1)
        return kpos <= tpos

    qit = qit_ref[...]
    wi_w = jnp.concatenate([qit[h * IDX_DIM:(h + 1) * IDX_DIM] for h in range(N_IDX_HEADS)], axis=1)
    wit = wit_ref[...]

    def score_block(j, _):
        rows = pl.ds(pl.multiple_of(j * kbs, kbs), kbs)
        d = _dot(kib_ref[rows, :], wi_w)
        sc = jnp.zeros((kbs, tq), F32)
        for h in range(N_IDX_HEADS):
            sc += jnp.maximum(d[:, h * tq:(h + 1) * tq], 0.0) * wit[h:h + 1, :]
        key_scr[rows, :] = jnp.where(causal(j, kbs), sc, -jnp.inf)
        return 0
    lax.fori_loop(0, nks, score_block, 0)

    def count(pred):
        def body(j, acc):
            blk = key_scr[pl.ds(pl.multiple_of(j * kbs, kbs), kbs), :]
            hit = jnp.where(pred(blk), 1.0, 0.0)
            return acc + jnp.sum(hit.reshape(kbs // 32, 32, tq), axis=0)
        acc = lax.fori_loop(0, nks, body, jnp.zeros((32, tq), F32))
        return _rows_sum(acc)

    thr = _kth_largest(lambda cand: count(lambda blk: blk >= cand), (1, tq), n_sel)

    @pl.when(jnp.max(count(lambda blk: blk >= thr)) > n_sel)
    def _():
        room = n_sel - count(lambda blk: blk > thr)
        kbt = tri_ref.shape[0]

        def demote(j, tie_seen):
            rows = pl.ds(pl.multiple_of(j * kbt, kbt), kbt)
            key = key_scr[rows, :]
            tie = jnp.where(key == thr, 1.0, 0.0)
            rank = tie_seen + _dot(tri_ref[...], tie.astype(BF16))
            key_scr[rows, :] = jnp.where((tie > 0.0) & (rank > room), -jnp.inf, key)
            return tie_seen + _rows_sum(tie)
        lax.fori_loop(0, (n_keys + kbt - 1) // kbt, demote, jnp.zeros((1, tq), F32))

    qt = qt_ref[...]
    zero = jnp.zeros((HEAD_DIM, tq), BF16)
    w_qk = []
    for n in range(N_KV_HEADS):
        cols = []
        for g in range(2):
            h = 2 * n + g
            qh = qt[h * HEAD_DIM:(h + 1) * HEAD_DIM]
            cols.append(jnp.concatenate([qh, zero] if n % 2 == 0 else [zero, qh], axis=0))
        w_qk.append(jnp.concatenate(cols, axis=1))

    sub = min(128, kba)

    def scores(kblk, bias, n, i):
        r = slice(i * sub, (i + 1) * sub)
        s = _dot(kblk[r, (n // 2) * LANES:(n // 2 + 1) * LANES], w_qk[n])
        return s + jnp.concatenate([bias[r], bias[r]], axis=1)

    def max_block(j, carry):
        rows = pl.ds(pl.multiple_of(j * kba, kba), kba)
        bias = jnp.where((key_scr[rows, :] >= thr) & causal(j, kba), 0.0, NEG)
        bias_scr[rows, :] = bias
        kblk = kb_ref[rows, :]
        out = []
        for n in range(N_KV_HEADS):
            mx = carry[n]
            for i in range(kba // sub):
                s = scores(kblk, bias, n, i)
                mx = jnp.maximum(mx, jnp.max(s.reshape(sub // 8, 8, 2 * tq), axis=0))
            out.append(mx)
        return tuple(out)
    mx = lax.fori_loop(0, nka, max_block, (jnp.full((8, 2 * tq), NEG, F32),) * N_KV_HEADS)
    m_row = [jnp.max(m, axis=0, keepdims=True) for m in mx]

    acc_scr[...] = jnp.zeros(acc_scr.shape, F32)

    def attend_block(j, carry):
        rows = pl.ds(pl.multiple_of(j * kba, kba), kba)
        bias = bias_scr[rows, :]
        kblk = kb_ref[rows, :]
        out = []
        for n in range(N_KV_HEADS):
            ls = carry[n]
            pieces = []
            for i in range(kba // sub):
                p = jnp.exp(scores(kblk, bias, n, i) - m_row[n])
                ls = ls + jnp.sum(p.reshape(sub // 8, 8, 2 * tq), axis=0)
                pieces.append(p.astype(BF16))
            acc_scr[n] += _dot(vt_ref[n * HEAD_DIM:(n + 1) * HEAD_DIM, rows], jnp.concatenate(pieces, axis=0))
            out.append(ls)
        return tuple(out)
    ls = lax.fori_loop(0, nka, attend_block, (jnp.zeros((8, 2 * tq), F32),) * N_KV_HEADS)

    parts = []
    for n in range(N_KV_HEADS):
        o = acc_scr[n] / jnp.sum(ls[n], axis=0, keepdims=True)
        parts += [o[:, :tq], o[:, tq:]]
    o_ref[...] = jnp.concatenate(parts, axis=0).T


def _upper_tri(n):
    r = np.arange(n)
    return jnp.asarray(r[:, None] <= r[None, :], BF16)


def _lower_tri(n):
    r = np.arange(n)
    return jnp.asarray(r[:, None] >= r[None, :], BF16)


def _dsa_prompt(qt, qit, wit, kb, kib, vt, b, s_len, tq=256, kbs=512, kba=512, kbt=256):
    kbs, kba, kbt = min(kbs, s_len), min(kba, s_len), min(kbt, s_len)
    n_sel = min(TOPK_KEYS, s_len // 4)
    nq = s_len // tq
    qspec = lambda r: pl.BlockSpec((r, tq), lambda bi, qi_: (0, bi * nq + qi_))
    kspec = lambda w: pl.BlockSpec((s_len, w), lambda bi, qi_: (bi, 0))
    return pl.pallas_call(
        functools.partial(_dsa_prompt_kernel, tq=tq, kbs=kbs, kba=kba, n_sel=n_sel),
        grid=(b, nq),
        in_specs=[qspec(512), qspec(256), qspec(8), kspec(256), kspec(IDX_DIM),
                  pl.BlockSpec((256, s_len), lambda bi, qi_: (0, bi)),
                  pl.BlockSpec((kbt, kbt), lambda bi, qi_: (0, 0))],
        out_specs=pl.BlockSpec((tq, D_ATT), lambda bi, qi_: (bi * nq + qi_, 0)),
        out_shape=jax.ShapeDtypeStruct((b * s_len, D_ATT), F32),
        scratch_shapes=[pltpu.VMEM((s_len, tq), F32),
                        pltpu.VMEM((s_len, tq), F32),
                        pltpu.VMEM((N_KV_HEADS, HEAD_DIM, 2 * tq), F32)],
        compiler_params=_cparams("arbitrary", "arbitrary"),
        name="dsa_prompt",
    )(qt, qit, wit, kb, kib, vt, _lower_tri(kbt))


def _dsa_sample_kernel(pt_ref, q_ref, qis_ref, wi_ref, knew_ref, vnew_ref, kinew_ref, tri_ref,
                       ck_ref, cv_ref, cki_ref, o_ref,
                       kibuf, kbuf, vbuf, sc_scr, sem_ki, sem_kv,
                       *, n_pages, n_sel, kbi, kba, ts):
    b = pl.program_id(0)
    nb = pl.num_programs(0)
    slot = b % 2
    past = n_pages * PAGE_SIZE
    rows = N_HEADS * ts
    hrows = rows // N_KV_HEADS

    def page_keys(p):
        return pl.ds(pl.multiple_of(p * PAGE_SIZE, PAGE_SIZE), PAGE_SIZE)

    def ki_copy(bb, sl, p):
        return pltpu.make_async_copy(cki_ref.at[pt_ref[bb, p]], kibuf.at[sl, :, page_keys(p)], sem_ki.at[sl])

    def k_copy(bb, p):
        return pltpu.make_async_copy(ck_ref.at[pt_ref[bb, p]], kbuf.at[:, :, page_keys(p)], sem_kv.at[0])

    def v_copy(bb, p):
        return pltpu.make_async_copy(cv_ref.at[pt_ref[bb, p]], vbuf.at[:, :, page_keys(p)], sem_kv.at[1])

    def for_pages(fn):
        def body(p, _):
            fn(p)
            return 0
        lax.fori_loop(0, n_pages, body, 0)

    @pl.when(b == 0)
    def _():
        for_pages(lambda p: ki_copy(b, slot, p).start())

    def start_kv(p):
        k_copy(b, p).start()
        v_copy(b, p).start()
    for_pages(start_kv)

    @pl.when(b + 1 < nb)
    def _():
        for_pages(lambda p: ki_copy(b + 1, 1 - slot, p).start())

    for_pages(lambda p: ki_copy(b, slot, p).wait())

    qis = qis_ref[...]
    wi = wi_ref[...]
    n_tiles = past // LANES + 1

    def index_scores(dots):
        d = jnp.maximum(dots, 0.0)
        sc = jnp.zeros((ts, dots.shape[1]), F32)
        for h in range(N_IDX_HEADS):
            sc += d[h * ts:(h + 1) * ts] * wi[:, h:h + 1]
        return sc

    def score_block(j, _):
        cols = pl.ds(pl.multiple_of(j * kbi, kbi), kbi)
        sc_scr[:, cols] = index_scores(_dot(qis, kibuf[slot, :, cols].astype(BF16)))
        return 0
    lax.fori_loop(0, past // kbi, score_block, 0)
    pad_rows = LANES - ts
    kin = jnp.concatenate([kinew_ref[...], jnp.zeros((pad_rows, IDX_DIM), F32)], axis=0)
    new_ok = (lax.broadcasted_iota(jnp.int32, (ts, LANES), 1)
              <= lax.broadcasted_iota(jnp.int32, (ts, LANES), 0))
    sc_scr[:, past:past + LANES] = jnp.where(new_ok, index_scores(_dot_nt(qis, kin.astype(BF16))), -jnp.inf)

    def counts(preds):
        sc = sc_scr[...]
        parts = [[jnp.zeros((ts, LANES), F32), jnp.zeros((ts, LANES), F32)] for _ in preds]
        for c in range(n_tiles):
            tile = sc[:, c * LANES:(c + 1) * LANES]
            for i, pred in enumerate(preds):
                parts[i][c % 2] = parts[i][c % 2] + jnp.where(pred(tile), 1.0, 0.0)
        return [jnp.sum(a + b_, axis=-1, keepdims=True) for a, b_ in parts]

    thr = _kth_largest_radix4(lambda c1, c2, c3: counts([lambda s: s >= c1, lambda s: s >= c2, lambda s: s >= c3]),
                              (ts, 1), n_sel)
    n_ge, n_gt = counts([lambda s: s >= thr, lambda s: s > thr])

    @pl.when(jnp.max(n_ge) > n_sel)
    def _():
        room = n_sel - n_gt
        kbt = tri_ref.shape[0]

        def demote(cols, tie_seen, tri):
            sc = sc_scr[:, cols]
            tie = jnp.where(sc == thr, 1.0, 0.0)
            rank = tie_seen + _dot(tie.astype(BF16), tri)
            sc_scr[:, cols] = jnp.where((tie > 0.0) & (rank > room), -jnp.inf, sc)
            return tie_seen + jnp.sum(tie, axis=-1, keepdims=True)
        seen = lax.fori_loop(0, past // kbt,
                             lambda j, s: demote(pl.ds(pl.multiple_of(j * kbt, kbt), kbt), s, tri_ref[...]),
                             jnp.zeros((ts, 1), F32))
        demote(pl.ds(past, LANES), seen, tri_ref[:LANES, :LANES])

    sc_scr[...] = jnp.where(sc_scr[...] >= thr, 0.0, NEG)
    sc_scr[:, past:past + LANES] = jnp.where(new_ok, sc_scr[:, past:past + LANES], NEG)

    for_pages(lambda p: k_copy(b, p).wait())
    for_pages(lambda p: v_copy(b, p).wait())

    q = q_ref[...]
    zpad = jnp.zeros((pad_rows, HEAD_DIM), F32)

    def key_block(j):
        return pl.ds(pl.multiple_of(j * kba, kba), kba)

    def head_rows(x, n):
        return x[n * hrows:(n + 1) * hrows]

    def scores(dots_of_head, bias):
        s = jnp.concatenate([dots_of_head(head_rows(q, n), n) for n in range(N_KV_HEADS)], axis=0)
        return s + jnp.concatenate([bias] * N_HEADS, axis=0)

    def block_scores(j):
        return scores(lambda qn, n: _dot(qn, kbuf[n, :, key_block(j)].astype(BF16)), sc_scr[:, key_block(j)])

    def new_scores():
        return scores(lambda qn, n: _dot_nt(qn, jnp.concatenate([knew_ref[n], zpad], axis=0).astype(BF16)),
                      sc_scr[:, past:past + LANES])

    def lane_tiles(x, op, acc):
        for c in range(x.shape[1] // LANES):
            acc = op(acc, x[:, c * LANES:(c + 1) * LANES])
        return acc

    mx = lax.fori_loop(0, past // kba, lambda j, m: lane_tiles(block_scores(j), jnp.maximum, m),
                       jnp.full((rows, LANES), NEG, F32))
    m_row = jnp.max(jnp.maximum(mx, new_scores()), axis=-1, keepdims=True)

    def value_product(p, pv_of_head, accs):
        return tuple(accs[n] + pv_of_head(head_rows(p, n).astype(BF16), n) for n in range(N_KV_HEADS))

    def attend_block(j, carry):
        ls, accs = carry
        p = jnp.exp(block_scores(j) - m_row)
        accs = value_product(p, lambda pn, n: _dot_nt(pn, vbuf[n, :, key_block(j)].astype(BF16)), accs)
        return lane_tiles(p, jnp.add, ls), accs
    zero_acc = (jnp.zeros((hrows, HEAD_DIM), F32),) * N_KV_HEADS
    ls, accs = lax.fori_loop(0, past // kba, attend_block, (jnp.zeros((rows, LANES), F32), zero_acc))
    p_new = jnp.exp(new_scores() - m_row)
    accs = value_product(
        p_new, lambda pn, n: _dot(pn, jnp.concatenate([vnew_ref[n], zpad], axis=0).astype(BF16)), accs)
    l_row = jnp.sum(ls + p_new, axis=-1, keepdims=True)
    o_ref[...] = jnp.concatenate(accs, axis=0) / l_row


def _dsa_sample(q, qi, wi, k, v, ki, cache_k, cache_v, cache_kidx, page_table, ts, kbi=2048, kba=1024, kbt=512):
    bd, n_pages = page_table.shape
    past = n_pages * PAGE_SIZE
    kbi, kba, kbt = min(kbi, past), min(kba, past), min(kbt, past)
    n_sel = min(TOPK_KEYS, (past + ts) // 4)
    rows = N_HEADS * ts
    heads_first = lambda a, nh: a.reshape(bd, ts, nh, -1).transpose(0, 2, 1, 3)
    qh = heads_first(q, N_HEADS).reshape(bd, rows, HEAD_DIM)
    qis = heads_first(qi, N_IDX_HEADS).reshape(bd, N_IDX_HEADS * ts, IDX_DIM)
    per_b = lambda *s: pl.BlockSpec((None,) + s, lambda i, pt: (i,) + (0,) * len(s))
    any_spec = pl.BlockSpec(memory_space=pl.ANY)
    page_buf = pltpu.VMEM((N_KV_HEADS, HEAD_DIM, past), F32)
    o = pl.pallas_call(
        functools.partial(_dsa_sample_kernel, n_pages=n_pages, n_sel=n_sel, kbi=kbi, kba=kba, ts=ts),
        grid_spec=pltpu.PrefetchScalarGridSpec(
            num_scalar_prefetch=1,
            grid=(bd,),
            in_specs=[per_b(rows, HEAD_DIM), per_b(N_IDX_HEADS * ts, IDX_DIM), per_b(ts, N_IDX_HEADS),
                      per_b(N_KV_HEADS, ts, HEAD_DIM), per_b(N_KV_HEADS, ts, HEAD_DIM), per_b(ts, IDX_DIM),
                      pl.BlockSpec((kbt, kbt), lambda i, pt: (0, 0)),
                      any_spec, any_spec, any_spec],
            out_specs=per_b(rows, HEAD_DIM),
            scratch_shapes=[pltpu.VMEM((2, IDX_DIM, past), F32), page_buf, page_buf,
                            pltpu.VMEM((ts, past + LANES), F32),
                            pltpu.SemaphoreType.DMA((2,)),
                            pltpu.SemaphoreType.DMA((2,))]),
        out_shape=jax.ShapeDtypeStruct((bd, rows, HEAD_DIM), F32),
        compiler_params=_cparams("arbitrary"),
        name="dsa_sample",
    )(page_table, qh, qis, wi.reshape(bd, ts, N_IDX_HEADS), heads_first(k, N_KV_HEADS),
      heads_first(v, N_KV_HEADS), ki.reshape(bd, ts, IDX_DIM), _upper_tri(kbt),
      cache_k.transpose(0, 2, 3, 1), cache_v.transpose(0, 2, 3, 1), cache_kidx.transpose(0, 2, 1))
    return o.reshape(bd, N_HEADS, ts, HEAD_DIM).transpose(0, 2, 1, 3).reshape(bd * ts, D_ATT)


def _layer_norm(x, g, b):
    mu = jnp.mean(x, axis=-1, keepdims=True)
    xc = x - mu
    var = jnp.mean(xc * xc, axis=-1, keepdims=True)
    return xc * lax.rsqrt(var + LN_EPS) * g + b


def _rms_norm(x, g):
    return x * lax.rsqrt(jnp.mean(x * x, axis=-1, keepdims=True) + LN_EPS) * g


def _mix_kernel(*refs):
    ys_refs, refs = refs[:S5_BLOCKS], refs[S5_BLOCKS:]
    (ya_ref, x_ref, g1_ref, sc2_ref, sh2_ref, wglu_ref, bglu_ref, gs_ref, ga_ref,
     wout_ref, l1g_ref, l1b_ref, wr_ref, br_ref, tri_ref,
     x1_ref, h2_ref, eidx_ref, gate_ref, rank_ref, cnt_ref, run_scr) = refs
    g = jax.nn.gelu(jnp.concatenate([r[...] for r in ys_refs], axis=-1))
    ssm = g * jax.nn.sigmoid(_dot(g.astype(BF16), wglu_ref[...]) + bglu_ref[...])
    cat = jnp.concatenate([_rms_norm(ssm, gs_ref[...]), _rms_norm(ya_ref[...], ga_ref[...])], axis=-1)
    mixed = _dot(cat.astype(BF16), wout_ref[...])
    x1 = _layer_norm(ALPHA * x_ref[...] + g1_ref[...] * mixed, l1g_ref[...], l1b_ref[...])
    x1_ref[...] = x1
    h2f = x1 * (1.0 + sc2_ref[...]) + sh2_ref[...]
    h2_ref[...] = h2f
    h2 = h2f.astype(BF16)
    aff = jax.nn.sigmoid(_dot(h2, wr_ref[...]))
    vals = aff + br_ref[...]
    tm = aff.shape[0]
    lane = lax.broadcasted_iota(jnp.int32, (tm, N_EXPERTS), 1)
    col = lax.broadcasted_iota(jnp.int32, (tm, 8), 1)
    eidx = jnp.zeros((tm, 8), jnp.int32)
    gsel = jnp.zeros((tm, 8), F32)
    chosen = jnp.zeros((tm, N_EXPERTS), F32)
    picks = []
    for j in range(TOP_K_EXPERTS):
        top = jnp.max(vals, axis=-1, keepdims=True)
        idx = jnp.min(jnp.where(vals == top, lane, N_EXPERTS), axis=-1, keepdims=True)
        hit = lane == idx
        gj = jnp.sum(jnp.where(hit, aff, 0.0), axis=-1, keepdims=True)
        eidx = jnp.where(col == j, idx, eidx)
        gsel = jnp.where(col == j, gj, gsel)
        vals = jnp.where(hit, -jnp.inf, vals)
        chosen = jnp.where(hit, 1.0, chosen)
        picks.append(idx)
    eidx_ref[...] = eidx
    gate_ref[...] = gsel / jnp.sum(gsel, axis=-1, keepdims=True) * ROUTED_SCALE

    @pl.when(pl.program_id(0) == 0)
    def _():
        run_scr[...] = jnp.zeros(run_scr.shape, F32)
    before = run_scr[...] + _dot(tri_ref[...], chosen.astype(BF16))
    rank = jnp.zeros((tm, 8), jnp.int32)
    for j in range(TOP_K_EXPERTS):
        rj = jnp.sum(jnp.where(lane == picks[j], before, 0.0), axis=-1, keepdims=True)
        rank = jnp.where(col == j, rj.astype(jnp.int32), rank)
    rank_ref[...] = rank
    run_scr[...] += jnp.sum(chosen, axis=0, keepdims=True)
    cnt_ref[...] = jnp.broadcast_to(run_scr[...], cnt_ref.shape)


def _mixer_out(ys, ya, x, g1, sc2, sh2, w, rows_per_mod, tm):
    t = x.shape[0]
    ms = _mod_spec(g1, tm, rows_per_mod)
    row = lambda wd: pl.BlockSpec((tm, wd), lambda i: (i, 0))
    full = lambda a: pl.BlockSpec(a.shape, lambda i: (0, 0))
    r = np.arange(tm)
    strict_lower = jnp.asarray(r[None, :] < r[:, None], BF16)
    consts = [w['w_glu'], w['b_glu'], w['g_ssm_out'], w['g_att_out'], w['w_out'], w['ln1_g'], w['ln1_b'],
              w['w_router'], w['b_router'], strict_lower]
    return pl.pallas_call(
        _mix_kernel,
        grid=(t // tm,),
        in_specs=[row(LANES)] * S5_BLOCKS + [row(D_ATT), row(D_MODEL), ms, ms, ms]
                 + [full(a) for a in consts],
        out_specs=[row(D_MODEL), row(D_MODEL), row(8), row(8), row(8),
                   pl.BlockSpec((8, N_EXPERTS), lambda i: (0, 0))],
        out_shape=[jax.ShapeDtypeStruct((t, D_MODEL), F32), jax.ShapeDtypeStruct((t, D_MODEL), F32),
                   jax.ShapeDtypeStruct((t, 8), jnp.int32), jax.ShapeDtypeStruct((t, 8), F32),
                   jax.ShapeDtypeStruct((t, 8), jnp.int32), jax.ShapeDtypeStruct((8, N_EXPERTS), F32)],
        scratch_shapes=[pltpu.VMEM((1, N_EXPERTS), F32)],
        compiler_params=_cparams("arbitrary"),
        name="mixer_out",
    )(*ys, ya, x, g1, sc2, sh2, *consts)


MOE_ROWS = 512


def _experts_kernel(be_ref, nused_ref, x_ref, w1_ref, w3_ref, w2_ref, y_ref):
    i = pl.program_id(0)

    @pl.when(i < nused_ref[0])
    def _():
        x = x_ref[...].astype(BF16)
        h1 = _dot(x, w1_ref[...].astype(BF16))
        h3 = _dot(x, w3_ref[...].astype(BF16))
        a = (h1 * jax.nn.sigmoid(h1) * h3).astype(BF16)
        y_ref[...] = _dot(a, w2_ref[...].astype(BF16)).astype(y_ref.dtype)

    @pl.when(i >= nused_ref[0])
    def _():
        y_ref[...] = jnp.zeros(y_ref.shape, y_ref.dtype)


def _experts(xs, block_e, n_used, w_e1, w_e3, w_e2):
    n_slots = xs.shape[0]
    n_blocks = n_slots // MOE_ROWS
    wspec = lambda r, c: pl.BlockSpec((None, r, c), lambda i, be, nu: (be[i], 0, 0))
    return pl.pallas_call(
        _experts_kernel,
        grid_spec=pltpu.PrefetchScalarGridSpec(
            num_scalar_prefetch=2,
            grid=(n_blocks,),
            in_specs=[pl.BlockSpec((MOE_ROWS, D_MODEL), lambda i, be, nu: (i, 0)),
                      wspec(D_MODEL, D_EXPERT), wspec(D_MODEL, D_EXPERT), wspec(D_EXPERT, D_MODEL)],
            out_specs=pl.BlockSpec((MOE_ROWS, D_MODEL), lambda i, be, nu: (i, 0))),
        out_shape=jax.ShapeDtypeStruct((n_slots, D_MODEL), F32),
        compiler_params=_cparams("arbitrary"),
        name="routed_experts",
    )(block_e, n_used, xs, w_e1, w_e3, w_e2)


def _dispatch_plan(eidx, rank, counts, n_tok):
    padded = (counts + MOE_ROWS - 1) // MOE_ROWS * MOE_ROWS
    pends = jnp.cumsum(padded)
    pstarts = pends - padded
    slot = pstarts[eidx] + rank
    n_blocks = -(-n_tok * TOP_K_EXPERTS // MOE_ROWS) + N_EXPERTS
    first_row = jnp.arange(n_blocks, dtype=jnp.int32) * MOE_ROWS
    block_e = jnp.sum((pends[None, :] <= first_row[:, None]).astype(jnp.int32), axis=1)
    block_e = jnp.minimum(block_e, N_EXPERTS - 1)
    n_used = (pends[-1:] // MOE_ROWS).astype(jnp.int32)
    return slot, block_e, n_used, n_blocks * MOE_ROWS


def _final_kernel(*refs):
    ye_refs = refs[:TOP_K_EXPERTS]
    gate_ref, h2_ref, x1_ref, g2_ref, ws1_ref, ws3_ref, ws2_ref, l2g_ref, l2b_ref, o_ref = refs[TOP_K_EXPERTS:]
    h2 = h2_ref[...].astype(BF16)
    h1 = _dot(h2, ws1_ref[...])
    h3 = _dot(h2, ws3_ref[...])
    a = (h1 * jax.nn.sigmoid(h1) * h3).astype(BF16)
    gate = gate_ref[...]
    routed = ye_refs[0][...].astype(F32) * gate[:, 0:1]
    for j in range(1, TOP_K_EXPERTS):
        routed += ye_refs[j][...].astype(F32) * gate[:, j:j + 1]
    ffn = routed + _dot(a, ws2_ref[...])
    o_ref[...] = _layer_norm(ALPHA * x1_ref[...] + g2_ref[...] * ffn, l2g_ref[...], l2b_ref[...])


def _final(ye_parts, gates, h2, x1, g2, w, rows_per_mod, tm):
    t = x1.shape[0]
    row = lambda wd: pl.BlockSpec((tm, wd), lambda i: (i, 0))
    full = lambda a: pl.BlockSpec(a.shape, lambda i: (0, 0))
    consts = [w['w_s1'], w['w_s3'], w['w_s2'], w['ln2_g'], w['ln2_b']]
    return pl.pallas_call(
        _final_kernel,
        grid=(t // tm,),
        in_specs=[row(D_MODEL)] * TOP_K_EXPERTS
                 + [row(8), row(D_MODEL), row(D_MODEL), _mod_spec(g2, tm, rows_per_mod)]
                 + [full(a) for a in consts],
        out_specs=row(D_MODEL),
        out_shape=jax.ShapeDtypeStruct((t, D_MODEL), F32),
        compiler_params=_cparams("arbitrary"),
        name="final_residual",
    )(*ye_parts, gates, h2, x1, g2, *consts)


def kernel(x_prompt, x_sample, c_prompt, c_sample, cache_k, cache_v, cache_kidx, state_ssm_re, state_ssm_im, page_table, w_ada, b_ada, w_in, ssm_a_re, ssm_a_im, ssm_log_dt, ssm_b_re, ssm_b_im, ssm_c_re, ssm_c_im, ssm_d, w_glu, b_glu, g_ssm_out, g_att_out, w_out, ln1_g, ln1_b, w_router, b_router, w_e1, w_e3, w_e2, w_s1, w_s3, w_s2, ln2_g, ln2_b):
    assert w_in.shape[0] == 1, "one layer"
    b, s, d = x_prompt.shape
    bd, ts, _ = x_sample.shape
    tp, tsn = b * s, bd * ts
    tm = 512
    row = lambda a: a[0].reshape(1, -1)
    w = dict(w_glu=w_glu[0].astype(BF16), b_glu=row(b_glu), g_ssm_out=row(g_ssm_out), g_att_out=row(g_att_out),
             w_out=w_out[0].astype(BF16), ln1_g=row(ln1_g), ln1_b=row(ln1_b),
             w_router=w_router[0].astype(BF16), b_router=row(b_router),
             w_s1=w_s1[0].astype(BF16), w_s3=w_s3[0].astype(BF16), w_s2=w_s2[0].astype(BF16),
             ln2_g=row(ln2_g), ln2_b=row(ln2_b))
    w_sample, w_prompt, w_prompt_t = _projection_weights(w_in[0])
    ssm_args = (ssm_a_re[0], ssm_a_im[0], ssm_log_dt[0], ssm_b_re[0], ssm_b_im[0], ssm_c_re[0], ssm_c_im[0],
                ssm_d[0])

    mod = _ada_mod(jnp.concatenate([c_prompt, c_sample], axis=0), w_ada[0], b_ada[0])
    mod_p = mod[:b].reshape(b, 6, 1, d)
    mod_s = jnp.broadcast_to(mod[b:].reshape(bd, 1, 6, d), (bd, ts, 6, d)).reshape(tsn, 6, d)

    xp = x_prompt.reshape(tp, d)
    u, k_p, v_p, ki_p, kb, kib, qt, qit, vt, wit = _in_projection_prompt(
        xp, mod_p[:, 1], mod_p[:, 0], w_prompt, w_prompt_t, s, tm)
    mats = _ssm_matrices(*ssm_args)
    zero_state = jnp.zeros((b, N_SSM_GROUPS, SSM_STATE), F32)
    ys_p, re_p, im_p = _s5_scan(u.reshape(b, s, D_SSM), zero_state, zero_state, mats, True, min(s, 128))
    ya_p = _dsa_prompt(qt, qit, wit, kb, kib, vt, b, s)
    x1_p, h2_p, eidx_p, gate_p, rank_p, cnt_p = _mixer_out([y.reshape(tp, LANES) for y in ys_p], ya_p, xp,
                                                           mod_p[:, 2], mod_p[:, 4], mod_p[:, 3], w, s, tm)

    assert ts == S5_CHUNK, "each sample sequence is one S5 chunk"
    xs = x_sample.reshape(tsn, d)
    tms = min(tm, tsn)
    u, q, k_s, v_s, qi, ki_s, wi = _in_projection(xs, mod_s[:, 1], mod_s[:, 0], w_sample, None, tms)
    ys_s, re_s, im_s = _s5_scan(u.reshape(1, tsn, D_SSM), state_ssm_re[0], state_ssm_im[0], mats, False, tsn)
    ya_s = _dsa_sample(q, qi, wi, k_s, v_s, ki_s, cache_k[0], cache_v[0], cache_kidx[0], page_table, ts)
    x1_s, h2_s, eidx_s, gate_s, rank_s, cnt_s = _mixer_out([y.reshape(tsn, LANES) for y in ys_s], ya_s, xs,
                                                           mod_s[:, 2], mod_s[:, 4], mod_s[:, 3], w, None, tms)

    n_tok = tp + tsn
    cnt_p, cnt_s = cnt_p[0].astype(jnp.int32), cnt_s[0].astype(jnp.int32)
    eidx_p, eidx_s = eidx_p[:, :TOP_K_EXPERTS], eidx_s[:, :TOP_K_EXPERTS]
    eidx = jnp.concatenate([eidx_p, eidx_s], axis=0)
    rank = jnp.concatenate([rank_p[:, :TOP_K_EXPERTS], rank_s[:, :TOP_K_EXPERTS] + cnt_p[eidx_s]], axis=0)
    slot, block_e, n_used, n_slots = _dispatch_plan(eidx, rank, cnt_p + cnt_s, n_tok)
    tok = jnp.broadcast_to(jnp.arange(n_tok, dtype=jnp.int32)[:, None], slot.shape)
    slot_tok = jnp.zeros((n_slots,), jnp.int32).at[slot.reshape(-1)].set(tok.reshape(-1), unique_indices=True)
    h2 = jnp.concatenate([h2_p, h2_s], axis=0)
    ye = _experts(h2[slot_tok], block_e, n_used, w_e1[0], w_e3[0], w_e2[0])
    parts_p = [ye[slot[:tp, j]] for j in range(TOP_K_EXPERTS)]
    parts_s = [ye[slot[tp:, j]] for j in range(TOP_K_EXPERTS)]

    y_p = _final(parts_p, gate_p, h2_p, x1_p, mod_p[:, 5], w, s, tm)
    y_s = _final(parts_s, gate_s, h2_s, x1_s, mod_s[:, 5], w, None, tms)

    kvs = (1, b, s, N_KV_HEADS, HEAD_DIM)
    kvd = (1, bd, ts, N_KV_HEADS, HEAD_DIM)
    return (y_p.reshape(b, s, d), y_s.reshape(bd, ts, d),
            k_p.reshape(kvs), v_p.reshape(kvs), ki_p.reshape(1, b, s, IDX_DIM), re_p[None], im_p[None],
            k_s.reshape(kvd), v_s.reshape(kvd), ki_s.reshape(1, bd, ts, IDX_DIM), re_s[None], im_s[None])
```

```python
import functools
import math

import jax
import jax.numpy as jnp
import numpy as np
from jax import lax
from jax.experimental import pallas as pl
from jax.experimental.pallas import tpu as pltpu

D_MODEL = 1024
D_SSM = 512
SSM_GROUP = 16
N_SSM_GROUPS = 32
SSM_STATE = 64
D_ATT = 512
HEAD_DIM = 64
N_HEADS = 8
N_KV_HEADS = 4
N_IDX_HEADS = 4
IDX_DIM = 64
TOPK_KEYS = 256
N_EXPERTS = 64
TOP_K_EXPERTS = 6
D_EXPERT = 256
ROUTED_SCALE = 2.5
PAGE_SIZE = 128
ALPHA = 2.0 ** 0.25
LN_EPS = 1e-5

LANES = 128
VMEM_LIMIT = 56 * 1024 * 1024
NEG = -1e30
INT_MIN = -2 ** 31

F32 = jnp.float32
BF16 = jnp.bfloat16
HI = lax.Precision.HIGHEST


def _cparams(*sem):
    return pltpu.CompilerParams(dimension_semantics=sem, vmem_limit_bytes=VMEM_LIMIT)


def _dot(a, b):
    return jnp.dot(a, b, preferred_element_type=F32)


def _dot_nt(a, b):
    return lax.dot_general(a, b, (((1,), (1,)), ((), ())), preferred_element_type=F32)


def _ada_kernel(c_ref, w_ref, b_ref, o_ref):
    c = c_ref[...]
    s = (c * jax.nn.sigmoid(c)).astype(BF16)
    o_ref[...] = _dot(s, w_ref[...].astype(BF16)) + b_ref[...]


def _ada_mod(c, w_ada, b_ada):
    n, d = c.shape
    dn = w_ada.shape[1]
    tn = 1024
    return pl.pallas_call(
        _ada_kernel,
        grid=(dn // tn,),
        in_specs=[pl.BlockSpec((n, d), lambda j: (0, 0)),
                  pl.BlockSpec((d, tn), lambda j: (0, j)),
                  pl.BlockSpec((1, tn), lambda j: (0, j))],
        out_specs=pl.BlockSpec((n, tn), lambda j: (0, j)),
        out_shape=jax.ShapeDtypeStruct((n, dn), F32),
        compiler_params=_cparams("arbitrary"),
        name="ada_mod",
    )(c, w_ada, b_ada.reshape(1, dn))


_C_U, _C_Q, _C_K, _C_V, _C_QI, _C_KI, _C_WI, _C_END = 0, 512, 1024, 1280, 1536, 1792, 1920, 2048
_P_U, _P_K, _P_KI, _P_END = 0, 512, 768, 896
_T_Q, _T_QI, _T_V, _T_WI, _T_K, _T_KI, _T_END = 0, 512, 768, 1024, 1032, 1288, 1352


def _projection_weights(w_in):
    d = w_in.shape[0]
    cuts = np.cumsum([D_SSM, N_HEADS * HEAD_DIM, N_KV_HEADS * HEAD_DIM, N_KV_HEADS * HEAD_DIM,
                      N_IDX_HEADS * IDX_DIM, IDX_DIM])
    wu, wq, wk, wv, wqi, wki, wwi = jnp.split(w_in, cuts.tolist(), axis=1)
    wq, wqi, wwi = wq * HEAD_DIM ** -0.5, wqi * IDX_DIM ** -0.5, wwi * N_IDX_HEADS ** -0.5
    z = lambda n: jnp.zeros((d, n), w_in.dtype)
    sample = jnp.concatenate([wu, wq, wk, wv, wqi, wki, z(64), wwi, z(124)], axis=1)
    prompt = jnp.concatenate([wu, wk, wki, z(64)], axis=1)
    prompt_t = jnp.concatenate([wq, wqi, wv, wwi, z(4), wk, wki], axis=1).T
    return sample.astype(BF16), prompt.astype(BF16), prompt_t.astype(BF16)


def _inproj_kernel(x_ref, sc_ref, sh_ref, w_ref, u_ref, q_ref, k_ref, v_ref, qi_ref, ki_ref, wi_ref):
    h = (x_ref[...] * (1.0 + sc_ref[...]) + sh_ref[...]).astype(BF16)
    r = _dot(h, w_ref[...])
    u_ref[...] = r[:, _C_U:_C_Q]
    q_ref[...] = r[:, _C_Q:_C_K].astype(BF16)
    k_ref[...] = r[:, _C_K:_C_V]
    v_ref[...] = r[:, _C_V:_C_QI]
    qi_ref[...] = r[:, _C_QI:_C_KI].astype(BF16)
    ki_ref[...] = r[:, _C_KI:_C_KI + IDX_DIM]
    wi_ref[...] = r[:, _C_WI:_C_WI + N_IDX_HEADS]


def _inproj_prompt_kernel(x_ref, sc_ref, sh_ref, w_ref, wt_ref,
                          u_ref, kb_ref, kib_ref, qt_ref, qit_ref, vtb_ref, wit_ref, kt_ref, vt_ref, kit_ref):
    h = (x_ref[...] * (1.0 + sc_ref[...]) + sh_ref[...]).astype(BF16)
    r = _dot(h, w_ref[...])
    u_ref[...] = r[:, _P_U:_P_K]
    kb_ref[...] = r[:, _P_K:_P_KI].astype(BF16)
    kib_ref[...] = r[:, _P_KI:_P_KI + IDX_DIM].astype(BF16)
    rt = _dot_nt(wt_ref[...], h)
    qt_ref[...] = rt[_T_Q:_T_QI].astype(BF16)
    qit_ref[...] = rt[_T_QI:_T_V].astype(BF16)
    vt = rt[_T_V:_T_WI]
    vtb_ref[...] = vt.astype(BF16)
    wit_ref[...] = rt[_T_WI:_T_K]
    kt_ref[...] = rt[_T_K:_T_KI]
    vt_ref[...] = vt
    kit_ref[...] = rt[_T_KI:_T_END]


def _mod_spec(mod, tm, rows_per_mod):
    if rows_per_mod is None:
        return pl.BlockSpec((tm, D_MODEL), lambda i: (i, 0))
    tiles = rows_per_mod // tm
    return pl.BlockSpec((None, 1, D_MODEL), lambda i: (i // tiles, 0, 0))


def _in_projection(x, sc, sh, w_tok, rows_per_mod, tm):
    t = x.shape[0]
    widths = [(D_SSM, F32), (512, BF16), (256, F32), (256, F32), (256, BF16), (IDX_DIM, F32),
              (N_IDX_HEADS, F32)]
    ms = _mod_spec(sc, tm, rows_per_mod)
    return pl.pallas_call(
        _inproj_kernel,
        grid=(t // tm,),
        in_specs=[pl.BlockSpec((tm, D_MODEL), lambda i: (i, 0)), ms, ms,
                  pl.BlockSpec(w_tok.shape, lambda i: (0, 0))],
        out_specs=[pl.BlockSpec((tm, w), lambda i: (i, 0)) for w, _ in widths],
        out_shape=[jax.ShapeDtypeStruct((t, w), dt) for w, dt in widths],
        compiler_params=_cparams("arbitrary"),
        name="in_projection",
    )(x, sc, sh, w_tok)


def _in_projection_prompt(x, sc, sh, w_tok, w_feat, rows_per_mod, tm):
    t = x.shape[0]
    tiles = rows_per_mod // tm
    nb = t // rows_per_mod
    tok = [(D_SSM, F32), (256, BF16), (IDX_DIM, BF16)]
    feat = [(512, BF16), (256, BF16), (256, BF16), (8, F32)]
    seq = [256, 256, IDX_DIM]
    ms = _mod_spec(sc, tm, rows_per_mod)
    return pl.pallas_call(
        _inproj_prompt_kernel,
        grid=(t // tm,),
        in_specs=[pl.BlockSpec((tm, D_MODEL), lambda i: (i, 0)), ms, ms,
                  pl.BlockSpec(w_tok.shape, lambda i: (0, 0)), pl.BlockSpec(w_feat.shape, lambda i: (0, 0))],
        out_specs=[pl.BlockSpec((tm, w), lambda i: (i, 0)) for w, _ in tok]
                  + [pl.BlockSpec((r, tm), lambda i: (0, i)) for r, _ in feat]
                  + [pl.BlockSpec((None, r, tm), lambda i: (i // tiles, 0, i % tiles)) for r in seq],
        out_shape=[jax.ShapeDtypeStruct((t, w), dt) for w, dt in tok]
                  + [jax.ShapeDtypeStruct((r, t), dt) for r, dt in feat]
                  + [jax.ShapeDtypeStruct((nb, r, rows_per_mod), F32) for r in seq],
        compiler_params=_cparams("arbitrary"),
        name="in_projection_prompt",
    )(x, sc, sh, w_tok, w_feat)


S5_CHUNK = 8
S5_BLOCKS = D_SSM // LANES
S5_GROUPS_PER_BLOCK = LANES // SSM_GROUP
S5_PAIRS = N_SSM_GROUPS // 2


def _ssm_matrices(a_re, a_im, log_dt, b_re, b_im, c_re, c_im, d_skip):
    g, p = a_re.shape
    L = S5_CHUNK
    dt = jnp.exp(log_dt)[:, None]
    mag = jnp.exp(dt * a_re)
    abr, abi = mag * jnp.cos(dt * a_im), mag * jnp.sin(dt * a_im)
    den = a_re * a_re + a_im * a_im
    nr, ni = abr - 1.0, abi
    fr = (nr * a_re + ni * a_im) / den
    fi = (ni * a_re - nr * a_im) / den
    bbr = fr[..., None] * b_re - fi[..., None] * b_im
    bbi = fr[..., None] * b_im + fi[..., None] * b_re
    dd = jnp.arange(L + 1, dtype=F32)[:, None, None]
    pm = jnp.exp(dd * (dt * a_re)[None])
    pw_r = pm * jnp.cos(dd * (dt * a_im)[None])
    pw_i = pm * jnp.sin(dd * (dt * a_im)[None])
    ca_r = c_re[None] * pw_r[:, :, None, :] - c_im[None] * pw_i[:, :, None, :]
    ca_i = c_re[None] * pw_i[:, :, None, :] + c_im[None] * pw_r[:, :, None, :]
    kd = (jnp.einsum('dgop,gpi->dgoi', ca_r[:L], bbr, precision=HI)
          - jnp.einsum('dgop,gpi->dgoi', ca_i[:L], bbi, precision=HI))
    nblk, gblk, npair = S5_BLOCKS, S5_GROUPS_PER_BLOCK, S5_PAIRS
    ppb = npair // nblk
    kd_t = jnp.transpose(kd, (1, 0, 3, 2)).reshape(nblk, gblk, L, SSM_GROUP, SSM_GROUP)
    mtd = jnp.einsum('qgdio,gh->qdgiho', kd_t, jnp.eye(gblk, dtype=F32))
    mtd = mtd.reshape(nblk, L, LANES, LANES)
    sel = np.zeros((ppb, gblk, 2), np.float32)
    for r in range(ppb):
        for t in range(2):
            sel[r, 2 * r + t, t] = 1.0
    sel = jnp.asarray(sel)
    rev_r, rev_i = pw_r[:L][::-1], pw_i[:L][::-1]
    f_r = rev_r[..., None] * bbr[None] - rev_i[..., None] * bbi[None]
    f_i = rev_r[..., None] * bbi[None] + rev_i[..., None] * bbr[None]

    def place_in(f):
        f = f.reshape(L, nblk, ppb, 2, p, SSM_GROUP)
        f = jnp.einsum('mqrtPi,rgt->qrmgitP', f, sel)
        return f.reshape(npair, L, LANES, 2 * p).astype(BF16)

    def place_out(c):
        c = c.reshape(L, nblk, ppb, 2, SSM_GROUP, p)
        c = jnp.einsum('jqrtoP,rgt->qrjtPgo', c, sel)
        return c.reshape(npair, L, 2 * p, LANES).astype(BF16)

    return dict(mtd=mtd.astype(BF16), mf_r=place_in(f_r), mf_i=place_in(f_i),
                mg_r=place_out(ca_r[1:]), mg_i=place_out(-ca_i[1:]),
                al_r=pw_r[L].reshape(npair, 1, 2 * p), al_i=pw_i[L].reshape(npair, 1, 2 * p),
                dsk=d_skip.reshape(nblk, 1, LANES))


def _s5_kernel(*refs, nseq, nc, chained):
    u_refs, refs = refs[:S5_BLOCKS], refs[S5_BLOCKS:]
    (s0r_ref, s0i_ref, mtd_ref, mfr_ref, mfi_ref, mgr_ref, mgi_ref, alr_ref, ali_ref, dsk_ref), refs = \
        refs[:10], refs[10:]
    y_refs, refs = refs[:S5_BLOCKS], refs[S5_BLOCKS:]
    sr_ref, si_ref, xb_scr, inr_scr, ini_scr, str_scr, sti_scr, sbr_scr, sbi_scr, car_r, car_i = refs
    L = S5_CHUNK
    rows = nseq * nc
    ppb = S5_PAIRS // S5_BLOCKS

    def x_block(m, q):
        return u_refs[q][:, pl.ds(m, nc, stride=L), :].reshape(rows, LANES)

    for m in range(L):
        for q in range(S5_BLOCKS):
            xb_scr[m, q] = x_block(m, q).astype(BF16)

    for p in range(S5_PAIRS):
        q = p // ppb
        acc_r = _dot(xb_scr[0, q], mfr_ref[p, 0])
        acc_i = _dot(xb_scr[0, q], mfi_ref[p, 0])
        for m in range(1, L):
            acc_r += _dot(xb_scr[m, q], mfr_ref[p, m])
            acc_i += _dot(xb_scr[m, q], mfi_ref[p, m])
        inr_scr[p] = acc_r
        ini_scr[p] = acc_i

    if chained:
        @pl.when(pl.program_id(0) == 0)
        def _():
            car_r[...] = s0r_ref[...]
            car_i[...] = s0i_ref[...]

        def step(c, carry):
            idx = pl.ds(c, nseq, stride=nc)
            out = []
            for p in range(S5_PAIRS):
                sr, si = carry[2 * p], carry[2 * p + 1]
                str_scr[p, idx, :] = sr
                sti_scr[p, idx, :] = si
                alr, ali = alr_ref[p], ali_ref[p]
                out.append(alr * sr - ali * si + inr_scr[p, idx, :])
                out.append(alr * si + ali * sr + ini_scr[p, idx, :])
            return tuple(out)

        init = []
        for p in range(S5_PAIRS):
            init += [car_r[p], car_i[p]]
        end = lax.fori_loop(0, nc, step, tuple(init))
        for p in range(S5_PAIRS):
            car_r[p] = end[2 * p]
            car_i[p] = end[2 * p + 1]
        sr_ref[...] = car_r[...]
        si_ref[...] = car_i[...]
    else:
        for p in range(S5_PAIRS):
            sr, si = s0r_ref[p], s0i_ref[p]
            str_scr[p] = sr
            sti_scr[p] = si
            alr, ali = alr_ref[p], ali_ref[p]
            sr_ref[p] = alr * sr - ali * si + inr_scr[p]
            si_ref[p] = alr * si + ali * sr + ini_scr[p]

    for p in range(S5_PAIRS):
        sbr_scr[p] = str_scr[p].astype(BF16)
        sbi_scr[p] = sti_scr[p].astype(BF16)

    for q in range(S5_BLOCKS):
        for j in range(L):
            acc = _dot(xb_scr[j, q], mtd_ref[q, 0])
            for m in range(j):
                acc += _dot(xb_scr[m, q], mtd_ref[q, j - m])
            for r in range(ppb):
                p = q * ppb + r
                acc += _dot(sbr_scr[p], mgr_ref[p, j])
                acc += _dot(sbi_scr[p], mgi_ref[p, j])
            y = acc + dsk_ref[q] * x_block(j, q)
            y_refs[q][:, pl.ds(j, nc, stride=L), :] = y.reshape(nseq, nc, LANES)


def _s5_scan(u, s0_re, s0_im, mats, chained, tt):
    nseq, t, _ = u.shape
    n_states = s0_re.shape[0]
    L = S5_CHUNK
    nc = tt // L
    rows = nseq * nc
    assert rows == n_states or chained
    pairs = lambda s: s.reshape(n_states, S5_PAIRS, 2 * SSM_STATE).transpose(1, 0, 2)
    full = lambda a: pl.BlockSpec(a.shape, lambda i: (0,) * a.ndim)
    consts = [mats[k] for k in ('mtd', 'mf_r', 'mf_i', 'mg_r', 'mg_i', 'al_r', 'al_i', 'dsk')]
    state_shape = jax.ShapeDtypeStruct((S5_PAIRS, n_states, 2 * SSM_STATE), F32)
    state_spec = pl.BlockSpec((S5_PAIRS, n_states, 2 * SSM_STATE), lambda i: (0, 0, 0))
    pair_rows = lambda dt: pltpu.VMEM((S5_PAIRS, rows, LANES), dt)
    lane_block = lambda q: pl.BlockSpec((nseq, tt, LANES), lambda i: (0, i, q))
    out_block = pl.BlockSpec((nseq, tt, LANES), lambda i: (0, i, 0))
    *y, sr, si = pl.pallas_call(
        functools.partial(_s5_kernel, nseq=nseq, nc=nc, chained=chained),
        grid=(t // tt,),
        in_specs=[lane_block(q) for q in range(S5_BLOCKS)] + [state_spec, state_spec]
                 + [full(a) for a in consts],
        out_specs=[out_block] * S5_BLOCKS + [state_spec, state_spec],
        out_shape=[jax.ShapeDtypeStruct((nseq, t, LANES), F32)] * S5_BLOCKS + [state_shape, state_shape],
        scratch_shapes=[pltpu.VMEM((L, S5_BLOCKS, rows, LANES), BF16),
                        pair_rows(F32), pair_rows(F32), pair_rows(F32), pair_rows(F32),
                        pair_rows(BF16), pair_rows(BF16),
                        pltpu.VMEM((S5_PAIRS, n_states, 2 * SSM_STATE), F32),
                        pltpu.VMEM((S5_PAIRS, n_states, 2 * SSM_STATE), F32)],
        compiler_params=_cparams("arbitrary"),
        name="s5_scan",
    )(*([u] * S5_BLOCKS), pairs(s0_re), pairs(s0_im), *consts)
    unpair = lambda s: s.transpose(1, 0, 2).reshape(n_states, N_SSM_GROUPS, SSM_STATE)
    return y, unpair(sr), unpair(si)


KEY_NEG_INF = INT_MIN + 0x7FFFFF


def _key_to_float(x):
    return pltpu.bitcast(jnp.where(x < 0, x ^ jnp.int32(0x7FFFFFFF), x), F32)


def _threshold_of(x):
    return jnp.where(x <= KEY_NEG_INF, -jnp.inf, _key_to_float(x))


def _kth_largest(count_ge, shape, n_sel):
    def bit_step(i, x):
        cand = x + lax.shift_left(jnp.int32(1), 31 - i)
        return jnp.where(count_ge(_key_to_float(cand)) >= n_sel, cand, x)
    return _threshold_of(lax.fori_loop(0, 32, bit_step, jnp.full(shape, INT_MIN, jnp.int32)))


def _kth_largest_radix4(count_ge3, shape, n_sel):
    def step(i, x):
        one = lax.shift_left(jnp.int32(1), 30 - 2 * i)
        c1, c2, c3 = x + one, x + 2 * one, x + 3 * one
        n1, n2, n3 = count_ge3(_key_to_float(c1), _key_to_float(c2), _key_to_float(c3))
        return jnp.where(n3 >= n_sel, c3, jnp.where(n2 >= n_sel, c2, jnp.where(n1 >= n_sel, c1, x)))
    return _threshold_of(lax.fori_loop(0, 16, step, jnp.full(shape, INT_MIN, jnp.int32)))


def _rows_sum(x):
    r, c = x.shape
    return jnp.sum(jnp.sum(x.reshape(r // 8, 8, c), axis=0), axis=0, keepdims=True)


def _rows_max(x):
    r, c = x.shape
    return jnp.max(jnp.max(x.reshape(r // 8, 8, c), axis=0), axis=0, keepdims=True)


def _dsa_prompt_kernel(qt_ref, qit_ref, wit_ref, kb_ref, kib_ref, vt_ref, tri_ref, o_ref,
                       key_scr, bias_scr, acc_scr, *, tq, kbs, kba, n_sel):
    qtile = pl.program_id(1)
    n_keys = qtile * tq + tq
    nks = (n_keys + kbs - 1) // kbs
    nka = (n_keys + kba - 1) // kba

    def causal(j, kb):
        kpos = j * kb + lax.broadcasted_iota(jnp.int32, (kb, tq), 0)
        tpos = qtile * tq + lax.broadcasted_iota(jnp.int32, (kb, tq), 1)
        return kpos <= tpos

    qit = qit_ref[...]
    wi_w = jnp.concatenate([qit[h * IDX_DIM:(h + 1) * IDX_DIM] for h in range(N_IDX_HEADS)], axis=1)
    wit = wit_ref[...]

    def score_block(j, _):
        rows = pl.ds(pl.multiple_of(j * kbs, kbs), kbs)
        d = _dot(kib_ref[rows, :], wi_w)
        sc = jnp.zeros((kbs, tq), F32)
        for h in range(N_IDX_HEADS):
            sc += jnp.maximum(d[:, h * tq:(h + 1) * tq], 0.0) * wit[h:h + 1, :]
        key_scr[rows, :] = jnp.where(causal(j, kbs), sc, -jnp.inf)
        return 0
    lax.fori_loop(0, nka * (kba // kbs), score_block, 0)

    def count(pred):
        def body(j, acc):
            blk = key_scr[pl.ds(pl.multiple_of(j * kbs, kbs), kbs), :]
            hit = jnp.where(pred(blk), 1.0, 0.0)
            return acc + jnp.sum(hit.reshape(kbs // 32, 32, tq), axis=0)
        acc = lax.fori_loop(0, nks, body, jnp.zeros((32, tq), F32))
        return _rows_sum(acc)

    thr = _kth_largest(lambda cand: count(lambda blk: blk >= cand), (1, tq), n_sel)

    @pl.when(jnp.max(count(lambda blk: blk >= thr)) > n_sel)
    def _():
        room = n_sel - count(lambda blk: blk > thr)
        kbt = tri_ref.shape[0]

        def demote(j, tie_seen):
            rows = pl.ds(pl.multiple_of(j * kbt, kbt), kbt)
            key = key_scr[rows, :]
            tie = jnp.where(key == thr, 1.0, 0.0)
            rank = tie_seen + _dot(tri_ref[...], tie.astype(BF16))
            key_scr[rows, :] = jnp.where((tie > 0.0) & (rank > room), -jnp.inf, key)
            return tie_seen + _rows_sum(tie)
        lax.fori_loop(0, (n_keys + kbt - 1) // kbt, demote, jnp.zeros((1, tq), F32))

    qt = qt_ref[...]
    zero = jnp.zeros((HEAD_DIM, tq), BF16)
    w_qk = []
    for n in range(N_KV_HEADS):
        cols = []
        for g in range(2):
            h = 2 * n + g
            qh = qt[h * HEAD_DIM:(h + 1) * HEAD_DIM]
            cols.append(jnp.concatenate([qh, zero] if n % 2 == 0 else [zero, qh], axis=0))
        w_qk.append(jnp.concatenate(cols, axis=1))

    sub = min(128, kba)

    def scores(kblk, bias, n, i):
        r = slice(i * sub, (i + 1) * sub)
        s = _dot(kblk[r, (n // 2) * LANES:(n // 2 + 1) * LANES], w_qk[n])
        return s + jnp.concatenate([bias[r], bias[r]], axis=1)

    def max_block(j, carry):
        rows = pl.ds(pl.multiple_of(j * kba, kba), kba)
        bias = jnp.where((key_scr[rows, :] >= thr) & causal(j, kba), 0.0, NEG)
        bias_scr[rows, :] = bias
        kblk = kb_ref[rows, :]
        out = []
        for n in range(N_KV_HEADS):
            mx = carry[n]
            for i in range(kba // sub):
                s = scores(kblk, bias, n, i)
                mx = jnp.maximum(mx, jnp.max(s.reshape(sub // 8, 8, 2 * tq), axis=0))
            out.append(mx)
        return tuple(out)
    mx = lax.fori_loop(0, nka, max_block, (jnp.full((8, 2 * tq), NEG, F32),) * N_KV_HEADS)
    m_row = [jnp.max(m, axis=0, keepdims=True) for m in mx]

    acc_scr[...] = jnp.zeros(acc_scr.shape, F32)

    def attend_block(j, carry):
        rows = pl.ds(pl.multiple_of(j * kba, kba), kba)
        bias = bias_scr[rows, :]
        kblk = kb_ref[rows, :]
        out = []
        for n in range(N_KV_HEADS):
            ls = carry[n]
            pieces = []
            for i in range(kba // sub):
                p = jnp.exp(scores(kblk, bias, n, i) - m_row[n])
                ls = ls + jnp.sum(p.reshape(sub // 8, 8, 2 * tq), axis=0)
                pieces.append(p.astype(BF16))
            acc_scr[n] += _dot(vt_ref[n * HEAD_DIM:(n + 1) * HEAD_DIM, rows], jnp.concatenate(pieces, axis=0))
            out.append(ls)
        return tuple(out)
    ls = lax.fori_loop(0, nka, attend_block, (jnp.zeros((8, 2 * tq), F32),) * N_KV_HEADS)

    parts = []
    for n in range(N_KV_HEADS):
        o = acc_scr[n] / jnp.sum(ls[n], axis=0, keepdims=True)
        parts += [o[:, :tq], o[:, tq:]]
    o_ref[...] = jnp.concatenate(parts, axis=0).T


def _upper_tri(n):
    r = np.arange(n)
    return jnp.asarray(r[:, None] <= r[None, :], BF16)


def _lower_tri(n):
    r = np.arange(n)
    return jnp.asarray(r[:, None] >= r[None, :], BF16)


def _dsa_prompt(qt, qit, wit, kb, kib, vt, b, s_len, tq=256, kbs=256, kba=512, kbt=256):
    kbs, kba, kbt = min(kbs, s_len), min(kba, s_len), min(kbt, s_len)
    n_sel = min(TOPK_KEYS, s_len // 4)
    nq = s_len // tq
    qspec = lambda r: pl.BlockSpec((r, tq), lambda bi, qi_: (0, bi * nq + qi_))
    kspec = lambda w: pl.BlockSpec((s_len, w), lambda bi, qi_: (bi, 0))
    return pl.pallas_call(
        functools.partial(_dsa_prompt_kernel, tq=tq, kbs=kbs, kba=kba, n_sel=n_sel),
        grid=(b, nq),
        in_specs=[qspec(512), qspec(256), qspec(8), kspec(256), kspec(IDX_DIM),
                  pl.BlockSpec((256, s_len), lambda bi, qi_: (0, bi)),
                  pl.BlockSpec((kbt, kbt), lambda bi, qi_: (0, 0))],
        out_specs=pl.BlockSpec((tq, D_ATT), lambda bi, qi_: (bi * nq + qi_, 0)),
        out_shape=jax.ShapeDtypeStruct((b * s_len, D_ATT), F32),
        scratch_shapes=[pltpu.VMEM((s_len, tq), F32),
                        pltpu.VMEM((s_len, tq), F32),
                        pltpu.VMEM((N_KV_HEADS, HEAD_DIM, 2 * tq), F32)],
        compiler_params=_cparams("arbitrary", "arbitrary"),
        name="dsa_prompt",
    )(qt, qit, wit, kb, kib, vt, _lower_tri(kbt))


def _dsa_sample_kernel(pt_ref, q_ref, qis_ref, wi_ref, knew_ref, vnew_ref, kinew_ref, tri_ref,
                       ck_ref, cv_ref, cki_ref, o_ref,
                       kibuf, kbuf, vbuf, sc_scr, sem_ki, sem_kv,
                       *, n_pages, n_sel, kbi, kba, ts):
    b = pl.program_id(0)
    nb = pl.num_programs(0)
    slot = b % 2
    past = n_pages * PAGE_SIZE
    rows = N_HEADS * ts
    hrows = rows // N_KV_HEADS

    def page_keys(p):
        return pl.ds(pl.multiple_of(p * PAGE_SIZE, PAGE_SIZE), PAGE_SIZE)

    def ki_copy(bb, sl, p):
        return pltpu.make_async_copy(cki_ref.at[pt_ref[bb, p]], kibuf.at[sl, :, page_keys(p)], sem_ki.at[sl])

    def k_copy(bb, p):
        return pltpu.make_async_copy(ck_ref.at[pt_ref[bb, p]], kbuf.at[:, :, page_keys(p)], sem_kv.at[0])

    def v_copy(bb, p):
        return pltpu.make_async_copy(cv_ref.at[pt_ref[bb, p]], vbuf.at[:, :, page_keys(p)], sem_kv.at[1])

    def for_pages(fn):
        def body(p, _):
            fn(p)
            return 0
        lax.fori_loop(0, n_pages, body, 0)

    @pl.when(b == 0)
    def _():
        for_pages(lambda p: ki_copy(b, slot, p).start())

    def start_kv(p):
        k_copy(b, p).start()
        v_copy(b, p).start()
    for_pages(start_kv)

    @pl.when(b + 1 < nb)
    def _():
        for_pages(lambda p: ki_copy(b + 1, 1 - slot, p).start())

    for_pages(lambda p: ki_copy(b, slot, p).wait())

    qis = qis_ref[...]
    wi = wi_ref[...]
    n_tiles = past // LANES + 1

    def index_scores(dots):
        d = jnp.maximum(dots, 0.0)
        sc = jnp.zeros((ts, dots.shape[1]), F32)
        for h in range(N_IDX_HEADS):
            sc += d[h * ts:(h + 1) * ts] * wi[:, h:h + 1]
        return sc

    def score_block(j, _):
        cols = pl.ds(pl.multiple_of(j * kbi, kbi), kbi)
        sc_scr[:, cols] = index_scores(_dot(qis, kibuf[slot, :, cols].astype(BF16)))
        return 0
    lax.fori_loop(0, past // kbi, score_block, 0)
    pad_rows = LANES - ts
    kin = jnp.concatenate([kinew_ref[...], jnp.zeros((pad_rows, IDX_DIM), F32)], axis=0)
    new_ok = (lax.broadcasted_iota(jnp.int32, (ts, LANES), 1)
              <= lax.broadcasted_iota(jnp.int32, (ts, LANES), 0))
    sc_scr[:, past:past + LANES] = jnp.where(new_ok, index_scores(_dot_nt(qis, kin.astype(BF16))), -jnp.inf)

    def counts(preds):
        sc = sc_scr[...]
        parts = [[jnp.zeros((ts, LANES), F32), jnp.zeros((ts, LANES), F32)] for _ in preds]
        for c in range(n_tiles):
            tile = sc[:, c * LANES:(c + 1) * LANES]
            for i, pred in enumerate(preds):
                parts[i][c % 2] = parts[i][c % 2] + jnp.where(pred(tile), 1.0, 0.0)
        return [jnp.sum(a + b_, axis=-1, keepdims=True) for a, b_ in parts]

    thr = _kth_largest_radix4(lambda c1, c2, c3: counts([lambda s: s >= c1, lambda s: s >= c2, lambda s: s >= c3]),
                              (ts, 1), n_sel)
    n_ge, n_gt = counts([lambda s: s >= thr, lambda s: s > thr])

    @pl.when(jnp.max(n_ge) > n_sel)
    def _():
        room = n_sel - n_gt
        kbt = tri_ref.shape[0]

        def demote(cols, tie_seen, tri):
            sc = sc_scr[:, cols]
            tie = jnp.where(sc == thr, 1.0, 0.0)
            rank = tie_seen + _dot(tie.astype(BF16), tri)
            sc_scr[:, cols] = jnp.where((tie > 0.0) & (rank > room), -jnp.inf, sc)
            return tie_seen + jnp.sum(tie, axis=-1, keepdims=True)
        seen = lax.fori_loop(0, past // kbt,
                             lambda j, s: demote(pl.ds(pl.multiple_of(j * kbt, kbt), kbt), s, tri_ref[...]),
                             jnp.zeros((ts, 1), F32))
        demote(pl.ds(past, LANES), seen, tri_ref[:LANES, :LANES])

    sc_scr[...] = jnp.where(sc_scr[...] >= thr, 0.0, NEG)
    sc_scr[:, past:past + LANES] = jnp.where(new_ok, sc_scr[:, past:past + LANES], NEG)

    for_pages(lambda p: k_copy(b, p).wait())
    for_pages(lambda p: v_copy(b, p).wait())

    q = q_ref[...]
    zpad = jnp.zeros((pad_rows, HEAD_DIM), F32)

    def key_block(j):
        return pl.ds(pl.multiple_of(j * kba, kba), kba)

    def head_rows(x, n):
        return x[n * hrows:(n + 1) * hrows]

    def scores(dots_of_head, bias):
        s = jnp.concatenate([dots_of_head(head_rows(q, n), n) for n in range(N_KV_HEADS)], axis=0)
        return s + jnp.concatenate([bias] * N_HEADS, axis=0)

    def block_scores(j):
        return scores(lambda qn, n: _dot(qn, kbuf[n, :, key_block(j)].astype(BF16)), sc_scr[:, key_block(j)])

    def new_scores():
        return scores(lambda qn, n: _dot_nt(qn, jnp.concatenate([knew_ref[n], zpad], axis=0).astype(BF16)),
                      sc_scr[:, past:past + LANES])

    def lane_tiles(x, op, acc):
        for c in range(x.shape[1] // LANES):
            acc = op(acc, x[:, c * LANES:(c + 1) * LANES])
        return acc

    mx = lax.fori_loop(0, past // kba, lambda j, m: lane_tiles(block_scores(j), jnp.maximum, m),
                       jnp.full((rows, LANES), NEG, F32))
    m_row = jnp.max(jnp.maximum(mx, new_scores()), axis=-1, keepdims=True)

    def value_product(p, pv_of_head, accs):
        return tuple(accs[n] + pv_of_head(head_rows(p, n).astype(BF16), n) for n in range(N_KV_HEADS))

    def attend_block(j, carry):
        ls, accs = carry
        p = jnp.exp(block_scores(j) - m_row)
        accs = value_product(p, lambda pn, n: _dot_nt(pn, vbuf[n, :, key_block(j)].astype(BF16)), accs)
        return lane_tiles(p, jnp.add, ls), accs
    zero_acc = (jnp.zeros((hrows, HEAD_DIM), F32),) * N_KV_HEADS
    ls, accs = lax.fori_loop(0, past // kba, attend_block, (jnp.zeros((rows, LANES), F32), zero_acc))
    p_new = jnp.exp(new_scores() - m_row)
    accs = value_product(
        p_new, lambda pn, n: _dot(pn, jnp.concatenate([vnew_ref[n], zpad], axis=0).astype(BF16)), accs)
    l_row = jnp.sum(ls + p_new, axis=-1, keepdims=True)
    o_ref[...] = jnp.concatenate(accs, axis=0) / l_row


def _dsa_sample(q, qi, wi, k, v, ki, cache_k, cache_v, cache_kidx, page_table, ts, kbi=2048, kba=1024, kbt=512):
    bd, n_pages = page_table.shape
    past = n_pages * PAGE_SIZE
    kbi, kba, kbt = min(kbi, past), min(kba, past), min(kbt, past)
    n_sel = min(TOPK_KEYS, (past + ts) // 4)
    rows = N_HEADS * ts
    heads_first = lambda a, nh: a.reshape(bd, ts, nh, -1).transpose(0, 2, 1, 3)
    qh = heads_first(q, N_HEADS).reshape(bd, rows, HEAD_DIM)
    qis = heads_first(qi, N_IDX_HEADS).reshape(bd, N_IDX_HEADS * ts, IDX_DIM)
    per_b = lambda *s: pl.BlockSpec((None,) + s, lambda i, pt: (i,) + (0,) * len(s))
    any_spec = pl.BlockSpec(memory_space=pl.ANY)
    page_buf = pltpu.VMEM((N_KV_HEADS, HEAD_DIM, past), F32)
    o = pl.pallas_call(
        functools.partial(_dsa_sample_kernel, n_pages=n_pages, n_sel=n_sel, kbi=kbi, kba=kba, ts=ts),
        grid_spec=pltpu.PrefetchScalarGridSpec(
            num_scalar_prefetch=1,
            grid=(bd,),
            in_specs=[per_b(rows, HEAD_DIM), per_b(N_IDX_HEADS * ts, IDX_DIM), per_b(ts, N_IDX_HEADS),
                      per_b(N_KV_HEADS, ts, HEAD_DIM), per_b(N_KV_HEADS, ts, HEAD_DIM), per_b(ts, IDX_DIM),
                      pl.BlockSpec((kbt, kbt), lambda i, pt: (0, 0)),
                      any_spec, any_spec, any_spec],
            out_specs=per_b(rows, HEAD_DIM),
            scratch_shapes=[pltpu.VMEM((2, IDX_DIM, past), F32), page_buf, page_buf,
                            pltpu.VMEM((ts, past + LANES), F32),
                            pltpu.SemaphoreType.DMA((2,)),
                            pltpu.SemaphoreType.DMA((2,))]),
        out_shape=jax.ShapeDtypeStruct((bd, rows, HEAD_DIM), F32),
        compiler_params=_cparams("arbitrary"),
        name="dsa_sample",
    )(page_table, qh, qis, wi.reshape(bd, ts, N_IDX_HEADS), heads_first(k, N_KV_HEADS),
      heads_first(v, N_KV_HEADS), ki.reshape(bd, ts, IDX_DIM), _upper_tri(kbt),
      cache_k.transpose(0, 2, 3, 1), cache_v.transpose(0, 2, 3, 1), cache_kidx.transpose(0, 2, 1))
    return o.reshape(bd, N_HEADS, ts, HEAD_DIM).transpose(0, 2, 1, 3).reshape(bd * ts, D_ATT)


def _layer_norm(x, g, b):
    mu = jnp.mean(x, axis=-1, keepdims=True)
    xc = x - mu
    var = jnp.mean(xc * xc, axis=-1, keepdims=True)
    return xc * lax.rsqrt(var + LN_EPS) * g + b


def _rms_norm(x, g):
    return x * lax.rsqrt(jnp.mean(x * x, axis=-1, keepdims=True) + LN_EPS) * g


def _mix_kernel(*refs):
    ys_refs, refs = refs[:S5_BLOCKS], refs[S5_BLOCKS:]
    (ya_ref, x_ref, g1_ref, sc2_ref, sh2_ref, wglu_ref, bglu_ref, gs_ref, ga_ref,
     wout_ref, l1g_ref, l1b_ref, wr_ref, br_ref, tri_ref,
     x1_ref, h2_ref, eidx_ref, gate_ref, rank_ref, cnt_ref, run_scr) = refs
    g = jax.nn.gelu(jnp.concatenate([r[...] for r in ys_refs], axis=-1))
    ssm = g * jax.nn.sigmoid(_dot(g.astype(BF16), wglu_ref[...]) + bglu_ref[...])
    cat = jnp.concatenate([_rms_norm(ssm, gs_ref[...]), _rms_norm(ya_ref[...], ga_ref[...])], axis=-1)
    mixed = _dot(cat.astype(BF16), wout_ref[...])
    x1 = _layer_norm(ALPHA * x_ref[...] + g1_ref[...] * mixed, l1g_ref[...], l1b_ref[...])
    x1_ref[...] = x1
    h2f = x1 * (1.0 + sc2_ref[...]) + sh2_ref[...]
    h2_ref[...] = h2f
    h2 = h2f.astype(BF16)
    aff = jax.nn.sigmoid(_dot(h2, wr_ref[...]))
    vals = aff + br_ref[...]
    tm = aff.shape[0]
    lane = lax.broadcasted_iota(jnp.int32, (tm, N_EXPERTS), 1)
    col = lax.broadcasted_iota(jnp.int32, (tm, 8), 1)
    eidx = jnp.zeros((tm, 8), jnp.int32)
    gsel = jnp.zeros((tm, 8), F32)
    chosen = jnp.zeros((tm, N_EXPERTS), F32)
    picks = []
    for j in range(TOP_K_EXPERTS):
        top = jnp.max(vals, axis=-1, keepdims=True)
        idx = jnp.min(jnp.where(vals == top, lane, N_EXPERTS), axis=-1, keepdims=True)
        hit = lane == idx
        gj = jnp.sum(jnp.where(hit, aff, 0.0), axis=-1, keepdims=True)
        eidx = jnp.where(col == j, idx, eidx)
        gsel = jnp.where(col == j, gj, gsel)
        vals = jnp.where(hit, -jnp.inf, vals)
        chosen = jnp.where(hit, 1.0, chosen)
        picks.append(idx)
    eidx_ref[...] = eidx
    gate_ref[...] = gsel / jnp.sum(gsel, axis=-1, keepdims=True) * ROUTED_SCALE

    @pl.when(pl.program_id(0) == 0)
    def _():
        run_scr[...] = jnp.zeros(run_scr.shape, F32)
    before = run_scr[...] + _dot(tri_ref[...], chosen.astype(BF16))
    rank = jnp.zeros((tm, 8), jnp.int32)
    for j in range(TOP_K_EXPERTS):
        rj = jnp.sum(jnp.where(lane == picks[j], before, 0.0), axis=-1, keepdims=True)
        rank = jnp.where(col == j, rj.astype(jnp.int32), rank)
    rank_ref[...] = rank
    run_scr[...] += jnp.sum(chosen, axis=0, keepdims=True)
    cnt_ref[...] = jnp.broadcast_to(run_scr[...], cnt_ref.shape)


def _mixer_out(ys, ya, x, g1, sc2, sh2, w, rows_per_mod, tm):
    t = x.shape[0]
    ms = _mod_spec(g1, tm, rows_per_mod)
    row = lambda wd: pl.BlockSpec((tm, wd), lambda i: (i, 0))
    full = lambda a: pl.BlockSpec(a.shape, lambda i: (0, 0))
    r = np.arange(tm)
    strict_lower = jnp.asarray(r[None, :] < r[:, None], BF16)
    consts = [w['w_glu'], w['b_glu'], w['g_ssm_out'], w['g_att_out'], w['w_out'], w['ln1_g'], w['ln1_b'],
              w['w_router'], w['b_router'], strict_lower]
    return pl.pallas_call(
        _mix_kernel,
        grid=(t // tm,),
        in_specs=[row(LANES)] * S5_BLOCKS + [row(D_ATT), row(D_MODEL), ms, ms, ms]
                 + [full(a) for a in consts],
        out_specs=[row(D_MODEL), row(D_MODEL), row(8), row(8), row(8),
                   pl.BlockSpec((8, N_EXPERTS), lambda i: (0, 0))],
        out_shape=[jax.ShapeDtypeStruct((t, D_MODEL), F32), jax.ShapeDtypeStruct((t, D_MODEL), F32),
                   jax.ShapeDtypeStruct((t, 8), jnp.int32), jax.ShapeDtypeStruct((t, 8), F32),
                   jax.ShapeDtypeStruct((t, 8), jnp.int32), jax.ShapeDtypeStruct((8, N_EXPERTS), F32)],
        scratch_shapes=[pltpu.VMEM((1, N_EXPERTS), F32)],
        compiler_params=_cparams("arbitrary"),
        name="mixer_out",
    )(*ys, ya, x, g1, sc2, sh2, *consts)


MOE_ROWS = 512


def _experts_kernel(be_ref, nused_ref, x_ref, w1_ref, w3_ref, w2_ref, y_ref):
    i = pl.program_id(0)

    @pl.when(i < nused_ref[0])
    def _():
        x = x_ref[...].astype(BF16)
        h1 = _dot(x, w1_ref[...].astype(BF16))
        h3 = _dot(x, w3_ref[...].astype(BF16))
        a = (h1 * jax.nn.sigmoid(h1) * h3).astype(BF16)
        y_ref[...] = _dot(a, w2_ref[...].astype(BF16)).astype(y_ref.dtype)

    @pl.when(i >= nused_ref[0])
    def _():
        y_ref[...] = jnp.zeros(y_ref.shape, y_ref.dtype)


def _experts(xs, block_e, n_used, w_e1, w_e3, w_e2):
    n_slots = xs.shape[0]
    n_blocks = n_slots // MOE_ROWS
    wspec = lambda r, c: pl.BlockSpec((None, r, c), lambda i, be, nu: (be[i], 0, 0))
    return pl.pallas_call(
        _experts_kernel,
        grid_spec=pltpu.PrefetchScalarGridSpec(
            num_scalar_prefetch=2,
            grid=(n_blocks,),
            in_specs=[pl.BlockSpec((MOE_ROWS, D_MODEL), lambda i, be, nu: (i, 0)),
                      wspec(D_MODEL, D_EXPERT), wspec(D_MODEL, D_EXPERT), wspec(D_EXPERT, D_MODEL)],
            out_specs=pl.BlockSpec((MOE_ROWS, D_MODEL), lambda i, be, nu: (i, 0))),
        out_shape=jax.ShapeDtypeStruct((n_slots, D_MODEL), F32),
        compiler_params=_cparams("arbitrary"),
        name="routed_experts",
    )(block_e, n_used, xs, w_e1, w_e3, w_e2)


def _dispatch_plan(eidx, rank, counts, n_tok):
    padded = (counts + MOE_ROWS - 1) // MOE_ROWS * MOE_ROWS
    pends = jnp.cumsum(padded)
    pstarts = pends - padded
    slot = pstarts[eidx] + rank
    n_blocks = -(-n_tok * TOP_K_EXPERTS // MOE_ROWS) + N_EXPERTS
    first_row = jnp.arange(n_blocks, dtype=jnp.int32) * MOE_ROWS
    block_e = jnp.sum((pends[None, :] <= first_row[:, None]).astype(jnp.int32), axis=1)
    block_e = jnp.minimum(block_e, N_EXPERTS - 1)
    n_used = (pends[-1:] // MOE_ROWS).astype(jnp.int32)
    return slot, block_e, n_used, n_blocks * MOE_ROWS


def _final_kernel(*refs):
    ye_refs = refs[:TOP_K_EXPERTS]
    gate_ref, h2_ref, x1_ref, g2_ref, ws1_ref, ws3_ref, ws2_ref, l2g_ref, l2b_ref, o_ref = refs[TOP_K_EXPERTS:]
    h2 = h2_ref[...].astype(BF16)
    h1 = _dot(h2, ws1_ref[...])
    h3 = _dot(h2, ws3_ref[...])
    a = (h1 * jax.nn.sigmoid(h1) * h3).astype(BF16)
    gate = gate_ref[...]
    routed = ye_refs[0][...].astype(F32) * gate[:, 0:1]
    for j in range(1, TOP_K_EXPERTS):
        routed += ye_refs[j][...].astype(F32) * gate[:, j:j + 1]
    ffn = routed + _dot(a, ws2_ref[...])
    o_ref[...] = _layer_norm(ALPHA * x1_ref[...] + g2_ref[...] * ffn, l2g_ref[...], l2b_ref[...])


def _final(ye_parts, gates, h2, x1, g2, w, rows_per_mod, tm):
    t = x1.shape[0]
    row = lambda wd: pl.BlockSpec((tm, wd), lambda i: (i, 0))
    full = lambda a: pl.BlockSpec(a.shape, lambda i: (0, 0))
    consts = [w['w_s1'], w['w_s3'], w['w_s2'], w['ln2_g'], w['ln2_b']]
    return pl.pallas_call(
        _final_kernel,
        grid=(t // tm,),
        in_specs=[row(D_MODEL)] * TOP_K_EXPERTS
                 + [row(8), row(D_MODEL), row(D_MODEL), _mod_spec(g2, tm, rows_per_mod)]
                 + [full(a) for a in consts],
        out_specs=row(D_MODEL),
        out_shape=jax.ShapeDtypeStruct((t, D_MODEL), F32),
        compiler_params=_cparams("arbitrary"),
        name="final_residual",
    )(*ye_parts, gates, h2, x1, g2, *consts)


def kernel(x_prompt, x_sample, c_prompt, c_sample, cache_k, cache_v, cache_kidx, state_ssm_re, state_ssm_im, page_table, w_ada, b_ada, w_in, ssm_a_re, ssm_a_im, ssm_log_dt, ssm_b_re, ssm_b_im, ssm_c_re, ssm_c_im, ssm_d, w_glu, b_glu, g_ssm_out, g_att_out, w_out, ln1_g, ln1_b, w_router, b_router, w_e1, w_e3, w_e2, w_s1, w_s3, w_s2, ln2_g, ln2_b):
    assert w_in.shape[0] == 1, "one layer"
    b, s, d = x_prompt.shape
    bd, ts, _ = x_sample.shape
    tp, tsn = b * s, bd * ts
    tm = 512
    row = lambda a: a[0].reshape(1, -1)
    w = dict(w_glu=w_glu[0].astype(BF16), b_glu=row(b_glu), g_ssm_out=row(g_ssm_out), g_att_out=row(g_att_out),
             w_out=w_out[0].astype(BF16), ln1_g=row(ln1_g), ln1_b=row(ln1_b),
             w_router=w_router[0].astype(BF16), b_router=row(b_router),
             w_s1=w_s1[0].astype(BF16), w_s3=w_s3[0].astype(BF16), w_s2=w_s2[0].astype(BF16),
             ln2_g=row(ln2_g), ln2_b=row(ln2_b))
    w_sample, w_prompt, w_prompt_t = _projection_weights(w_in[0])
    ssm_args = (ssm_a_re[0], ssm_a_im[0], ssm_log_dt[0], ssm_b_re[0], ssm_b_im[0], ssm_c_re[0], ssm_c_im[0],
                ssm_d[0])

    mod = _ada_mod(jnp.concatenate([c_prompt, c_sample], axis=0), w_ada[0], b_ada[0])
    mod_p = mod[:b].reshape(b, 6, 1, d)
    mod_s = jnp.broadcast_to(mod[b:].reshape(bd, 1, 6, d), (bd, ts, 6, d)).reshape(tsn, 6, d)

    xp = x_prompt.reshape(tp, d)
    u, kb, kib, qt, qit, vt, wit, kt_p, vt_p, kit_p = _in_projection_prompt(
        xp, mod_p[:, 1], mod_p[:, 0], w_prompt, w_prompt_t, s, tm)
    mats = _ssm_matrices(*ssm_args)
    zero_state = jnp.zeros((b, N_SSM_GROUPS, SSM_STATE), F32)
    ys_p, re_p, im_p = _s5_scan(u.reshape(b, s, D_SSM), zero_state, zero_state, mats, True, min(s, 128))
    ya_p = _dsa_prompt(qt, qit, wit, kb, kib, vt, b, s)
    x1_p, h2_p, eidx_p, gate_p, rank_p, cnt_p = _mixer_out([y.reshape(tp, LANES) for y in ys_p], ya_p, xp,
                                                           mod_p[:, 2], mod_p[:, 4], mod_p[:, 3], w, s, tm)

    assert ts == S5_CHUNK, "each sample sequence is one S5 chunk"
    xs = x_sample.reshape(tsn, d)
    tms = min(tm, tsn)
    u, q, k_s, v_s, qi, ki_s, wi = _in_projection(xs, mod_s[:, 1], mod_s[:, 0], w_sample, None, tms)
    ys_s, re_s, im_s = _s5_scan(u.reshape(1, tsn, D_SSM), state_ssm_re[0], state_ssm_im[0], mats, False, tsn)
    ya_s = _dsa_sample(q, qi, wi, k_s, v_s, ki_s, cache_k[0], cache_v[0], cache_kidx[0], page_table, ts)
    x1_s, h2_s, eidx_s, gate_s, rank_s, cnt_s = _mixer_out([y.reshape(tsn, LANES) for y in ys_s], ya_s, xs,
                                                           mod_s[:, 2], mod_s[:, 4], mod_s[:, 3], w, None, tms)

    n_tok = tp + tsn
    cnt_p, cnt_s = cnt_p[0].astype(jnp.int32), cnt_s[0].astype(jnp.int32)
    eidx_p, eidx_s = eidx_p[:, :TOP_K_EXPERTS], eidx_s[:, :TOP_K_EXPERTS]
    eidx = jnp.concatenate([eidx_p, eidx_s], axis=0)
    rank = jnp.concatenate([rank_p[:, :TOP_K_EXPERTS], rank_s[:, :TOP_K_EXPERTS] + cnt_p[eidx_s]], axis=0)
    slot, block_e, n_used, n_slots = _dispatch_plan(eidx, rank, cnt_p + cnt_s, n_tok)
    tok = jnp.broadcast_to(jnp.arange(n_tok, dtype=jnp.int32)[:, None], slot.shape)
    slot_tok = jnp.zeros((n_slots,), jnp.int32).at[slot.reshape(-1)].set(tok.reshape(-1))
    h2 = jnp.concatenate([h2_p, h2_s], axis=0)
    ye = _experts(h2[slot_tok], block_e, n_used, w_e1[0], w_e3[0], w_e2[0])
    parts_p = [ye[slot[:tp, j]] for j in range(TOP_K_EXPERTS)]
    parts_s = [ye[slot[tp:, j]] for j in range(TOP_K_EXPERTS)]

    y_p = _final(parts_p, gate_p, h2_p, x1_p, mod_p[:, 5], w, s, tm)
    y_s = _final(parts_s, gate_s, h2_s, x1_s, mod_s[:, 5], w, None, tms)

    heads_last = lambda a: a.reshape(b, N_KV_HEADS, HEAD_DIM, s).transpose(0, 3, 1, 2)[None]
    kvd = (1, bd, ts, N_KV_HEADS, HEAD_DIM)
    return (y_p.reshape(b, s, d), y_s.reshape(bd, ts, d),
            heads_last(kt_p), heads_last(vt_p), kit_p.transpose(0, 2, 1)[None], re_p[None], im_p[None],
            k_s.reshape(kvd), v_s.reshape(kvd), ki_s.reshape(1, bd, ts, IDX_DIM), re_s[None], im_s[None])
```

```python
import functools
import math

import jax
import jax.numpy as jnp
import numpy as np
from jax import lax
from jax.experimental import pallas as pl
from jax.experimental.pallas import tpu as pltpu

D_MODEL = 1024
D_SSM = 512
SSM_GROUP = 16
N_SSM_GROUPS = 32
SSM_STATE = 64
D_ATT = 512
HEAD_DIM = 64
N_HEADS = 8
N_KV_HEADS = 4
N_IDX_HEADS = 4
IDX_DIM = 64
TOPK_KEYS = 256
N_EXPERTS = 64
TOP_K_EXPERTS = 6
D_EXPERT = 256
ROUTED_SCALE = 2.5
PAGE_SIZE = 128
ALPHA = 2.0 ** 0.25
LN_EPS = 1e-5

LANES = 128
VMEM_LIMIT = 56 * 1024 * 1024
NEG = -1e30
INT_MIN = -2 ** 31

F32 = jnp.float32
BF16 = jnp.bfloat16
HI = lax.Precision.HIGHEST


def _cparams(*sem):
    return pltpu.CompilerParams(dimension_semantics=sem, vmem_limit_bytes=VMEM_LIMIT)


def _dot(a, b):
    return jnp.dot(a, b, preferred_element_type=F32)


def _dot_nt(a, b):
    return lax.dot_general(a, b, (((1,), (1,)), ((), ())), preferred_element_type=F32)


def _ada_kernel(c_ref, w_ref, b_ref, o_ref):
    c = c_ref[...]
    s = (c * jax.nn.sigmoid(c)).astype(BF16)
    o_ref[...] = _dot(s, w_ref[...].astype(BF16)) + b_ref[...]


def _ada_mod(c, w_ada, b_ada):
    n, d = c.shape
    dn = w_ada.shape[1]
    tn = 1024
    return pl.pallas_call(
        _ada_kernel,
        grid=(dn // tn,),
        in_specs=[pl.BlockSpec((n, d), lambda j: (0, 0)),
                  pl.BlockSpec((d, tn), lambda j: (0, j)),
                  pl.BlockSpec((1, tn), lambda j: (0, j))],
        out_specs=pl.BlockSpec((n, tn), lambda j: (0, j)),
        out_shape=jax.ShapeDtypeStruct((n, dn), F32),
        compiler_params=_cparams("arbitrary"),
        name="ada_mod",
    )(c, w_ada, b_ada.reshape(1, dn))


_C_U, _C_Q, _C_K, _C_V, _C_QI, _C_KI, _C_WI, _C_END = 0, 512, 1024, 1280, 1536, 1792, 1920, 2048
_P_U, _P_K, _P_KI, _P_END = 0, 512, 768, 896
_T_Q, _T_QI, _T_V, _T_WI, _T_K, _T_KI, _T_END = 0, 512, 768, 1024, 1032, 1288, 1352


def _projection_weights(w_in):
    d = w_in.shape[0]
    cuts = np.cumsum([D_SSM, N_HEADS * HEAD_DIM, N_KV_HEADS * HEAD_DIM, N_KV_HEADS * HEAD_DIM,
                      N_IDX_HEADS * IDX_DIM, IDX_DIM])
    wu, wq, wk, wv, wqi, wki, wwi = jnp.split(w_in, cuts.tolist(), axis=1)
    wq, wqi, wwi = wq * HEAD_DIM ** -0.5, wqi * IDX_DIM ** -0.5, wwi * N_IDX_HEADS ** -0.5
    z = lambda n: jnp.zeros((d, n), w_in.dtype)
    sample = jnp.concatenate([wu, wq, wk, wv, wqi, wki, z(64), wwi, z(124)], axis=1)
    prompt = jnp.concatenate([wu, wk, wki, z(64)], axis=1)
    prompt_t = jnp.concatenate([wq, wqi, wv, wwi, z(4), wk, wki], axis=1).T
    return sample.astype(BF16), prompt.astype(BF16), prompt_t.astype(BF16)


def _inproj_kernel(x_ref, sc_ref, sh_ref, w_ref, u_ref, q_ref, k_ref, v_ref, qi_ref, ki_ref, wi_ref):
    h = (x_ref[...] * (1.0 + sc_ref[...]) + sh_ref[...]).astype(BF16)
    r = _dot(h, w_ref[...])
    u_ref[...] = r[:, _C_U:_C_Q]
    q_ref[...] = r[:, _C_Q:_C_K].astype(BF16)
    k_ref[...] = r[:, _C_K:_C_V]
    v_ref[...] = r[:, _C_V:_C_QI]
    qi_ref[...] = r[:, _C_QI:_C_KI].astype(BF16)
    ki_ref[...] = r[:, _C_KI:_C_KI + IDX_DIM]
    wi_ref[...] = r[:, _C_WI:_C_WI + N_IDX_HEADS]


def _inproj_prompt_kernel(x_ref, sc_ref, sh_ref, w_ref, wt_ref,
                          u_ref, kb_ref, kib_ref, qt_ref, qit_ref, vtb_ref, wit_ref, kt_ref, vt_ref, kit_ref):
    h = (x_ref[...] * (1.0 + sc_ref[...]) + sh_ref[...]).astype(BF16)
    r = _dot(h, w_ref[...])
    u_ref[...] = r[:, _P_U:_P_K]
    kb_ref[...] = r[:, _P_K:_P_KI].astype(BF16)
    kib_ref[...] = r[:, _P_KI:_P_KI + IDX_DIM].astype(BF16)
    rt = _dot_nt(wt_ref[...], h)
    qt_ref[...] = rt[_T_Q:_T_QI].astype(BF16)
    qit_ref[...] = rt[_T_QI:_T_V].astype(BF16)
    vt = rt[_T_V:_T_WI]
    vtb_ref[...] = vt.astype(BF16)
    wit_ref[...] = rt[_T_WI:_T_K]
    kt_ref[...] = rt[_T_K:_T_KI]
    vt_ref[...] = vt
    kit_ref[...] = rt[_T_KI:_T_END]


def _mod_spec(mod, tm, rows_per_mod):
    if rows_per_mod is None:
        return pl.BlockSpec((tm, D_MODEL), lambda i: (i, 0))
    tiles = rows_per_mod // tm
    return pl.BlockSpec((None, 1, D_MODEL), lambda i: (i // tiles, 0, 0))


def _in_projection(x, sc, sh, w_tok, rows_per_mod, tm):
    t = x.shape[0]
    widths = [(D_SSM, F32), (512, BF16), (256, F32), (256, F32), (256, BF16), (IDX_DIM, F32),
              (N_IDX_HEADS, F32)]
    ms = _mod_spec(sc, tm, rows_per_mod)
    return pl.pallas_call(
        _inproj_kernel,
        grid=(t // tm,),
        in_specs=[pl.BlockSpec((tm, D_MODEL), lambda i: (i, 0)), ms, ms,
                  pl.BlockSpec(w_tok.shape, lambda i: (0, 0))],
        out_specs=[pl.BlockSpec((tm, w), lambda i: (i, 0)) for w, _ in widths],
        out_shape=[jax.ShapeDtypeStruct((t, w), dt) for w, dt in widths],
        compiler_params=_cparams("arbitrary"),
        name="in_projection",
    )(x, sc, sh, w_tok)


def _in_projection_prompt(x, sc, sh, w_tok, w_feat, rows_per_mod, tm):
    t = x.shape[0]
    tiles = rows_per_mod // tm
    nb = t // rows_per_mod
    tok = [(D_SSM, F32), (256, BF16), (IDX_DIM, BF16)]
    feat = [(512, BF16), (256, BF16), (256, BF16), (8, F32)]
    seq = [256, 256, IDX_DIM]
    ms = _mod_spec(sc, tm, rows_per_mod)
    return pl.pallas_call(
        _inproj_prompt_kernel,
        grid=(t // tm,),
        in_specs=[pl.BlockSpec((tm, D_MODEL), lambda i: (i, 0)), ms, ms,
                  pl.BlockSpec(w_tok.shape, lambda i: (0, 0)), pl.BlockSpec(w_feat.shape, lambda i: (0, 0))],
        out_specs=[pl.BlockSpec((tm, w), lambda i: (i, 0)) for w, _ in tok]
                  + [pl.BlockSpec((r, tm), lambda i: (0, i)) for r, _ in feat]
                  + [pl.BlockSpec((None, r, tm), lambda i: (i // tiles, 0, i % tiles)) for r in seq],
        out_shape=[jax.ShapeDtypeStruct((t, w), dt) for w, dt in tok]
                  + [jax.ShapeDtypeStruct((r, t), dt) for r, dt in feat]
                  + [jax.ShapeDtypeStruct((nb, r, rows_per_mod), F32) for r in seq],
        compiler_params=_cparams("arbitrary"),
        name="in_projection_prompt",
    )(x, sc, sh, w_tok, w_feat)


S5_CHUNK = 8
S5_BLOCKS = D_SSM // LANES
S5_GROUPS_PER_BLOCK = LANES // SSM_GROUP
S5_PAIRS = N_SSM_GROUPS // 2


def _ssm_matrices(a_re, a_im, log_dt, b_re, b_im, c_re, c_im, d_skip):
    g, p = a_re.shape
    L = S5_CHUNK
    dt = jnp.exp(log_dt)[:, None]
    mag = jnp.exp(dt * a_re)
    abr, abi = mag * jnp.cos(dt * a_im), mag * jnp.sin(dt * a_im)
    den = a_re * a_re + a_im * a_im
    nr, ni = abr - 1.0, abi
    fr = (nr * a_re + ni * a_im) / den
    fi = (ni * a_re - nr * a_im) / den
    bbr = fr[..., None] * b_re - fi[..., None] * b_im
    bbi = fr[..., None] * b_im + fi[..., None] * b_re
    dd = jnp.arange(L + 1, dtype=F32)[:, None, None]
    pm = jnp.exp(dd * (dt * a_re)[None])
    pw_r = pm * jnp.cos(dd * (dt * a_im)[None])
    pw_i = pm * jnp.sin(dd * (dt * a_im)[None])
    ca_r = c_re[None] * pw_r[:, :, None, :] - c_im[None] * pw_i[:, :, None, :]
    ca_i = c_re[None] * pw_i[:, :, None, :] + c_im[None] * pw_r[:, :, None, :]
    kd = (jnp.einsum('dgop,gpi->dgoi', ca_r[:L], bbr, precision=HI)
          - jnp.einsum('dgop,gpi->dgoi', ca_i[:L], bbi, precision=HI))
    nblk, gblk, npair = S5_BLOCKS, S5_GROUPS_PER_BLOCK, S5_PAIRS
    ppb = npair // nblk
    kd_t = jnp.transpose(kd, (1, 0, 3, 2)).reshape(nblk, gblk, L, SSM_GROUP, SSM_GROUP)
    mtd = jnp.einsum('qgdio,gh->qdgiho', kd_t, jnp.eye(gblk, dtype=F32))
    mtd = mtd.reshape(nblk, L, LANES, LANES)
    sel = np.zeros((ppb, gblk, 2), np.float32)
    for r in range(ppb):
        for t in range(2):
            sel[r, 2 * r + t, t] = 1.0
    sel = jnp.asarray(sel)
    rev_r, rev_i = pw_r[:L][::-1], pw_i[:L][::-1]
    f_r = rev_r[..., None] * bbr[None] - rev_i[..., None] * bbi[None]
    f_i = rev_r[..., None] * bbi[None] + rev_i[..., None] * bbr[None]

    def place_in(f):
        f = f.reshape(L, nblk, ppb, 2, p, SSM_GROUP)
        f = jnp.einsum('mqrtPi,rgt->qrmgitP', f, sel)
        return f.reshape(npair, L, LANES, 2 * p).astype(BF16)

    def place_out(c):
        c = c.reshape(L, nblk, ppb, 2, SSM_GROUP, p)
        c = jnp.einsum('jqrtoP,rgt->qrjtPgo', c, sel)
        return c.reshape(npair, L, 2 * p, LANES).astype(BF16)

    return dict(mtd=mtd.astype(BF16), mf_r=place_in(f_r), mf_i=place_in(f_i),
                mg_r=place_out(ca_r[1:]), mg_i=place_out(-ca_i[1:]),
                al_r=pw_r[L].reshape(npair, 1, 2 * p), al_i=pw_i[L].reshape(npair, 1, 2 * p),
                dsk=d_skip.reshape(nblk, 1, LANES))


def _s5_kernel(*refs, nseq, nc, chained):
    u_refs, refs = refs[:S5_BLOCKS], refs[S5_BLOCKS:]
    (s0r_ref, s0i_ref, mtd_ref, mfr_ref, mfi_ref, mgr_ref, mgi_ref, alr_ref, ali_ref, dsk_ref), refs = \
        refs[:10], refs[10:]
    y_refs, refs = refs[:S5_BLOCKS], refs[S5_BLOCKS:]
    sr_ref, si_ref, xb_scr, inr_scr, ini_scr, str_scr, sti_scr, sbr_scr, sbi_scr, car_r, car_i = refs
    L = S5_CHUNK
    rows = nseq * nc
    ppb = S5_PAIRS // S5_BLOCKS

    def x_block(m, q):
        return u_refs[q][:, pl.ds(m, nc, stride=L), :].reshape(rows, LANES)

    for m in range(L):
        for q in range(S5_BLOCKS):
            xb_scr[m, q] = x_block(m, q).astype(BF16)

    for p in range(S5_PAIRS):
        q = p // ppb
        acc_r = _dot(xb_scr[0, q], mfr_ref[p, 0])
        acc_i = _dot(xb_scr[0, q], mfi_ref[p, 0])
        for m in range(1, L):
            acc_r += _dot(xb_scr[m, q], mfr_ref[p, m])
            acc_i += _dot(xb_scr[m, q], mfi_ref[p, m])
        inr_scr[p] = acc_r
        ini_scr[p] = acc_i

    if chained:
        @pl.when(pl.program_id(0) == 0)
        def _():
            car_r[...] = s0r_ref[...]
            car_i[...] = s0i_ref[...]

        def step(c, carry):
            idx = pl.ds(c, nseq, stride=nc)
            out = []
            for p in range(S5_PAIRS):
                sr, si = carry[2 * p], carry[2 * p + 1]
                str_scr[p, idx, :] = sr
                sti_scr[p, idx, :] = si
                alr, ali = alr_ref[p], ali_ref[p]
                out.append(alr * sr - ali * si + inr_scr[p, idx, :])
                out.append(alr * si + ali * sr + ini_scr[p, idx, :])
            return tuple(out)

        init = []
        for p in range(S5_PAIRS):
            init += [car_r[p], car_i[p]]
        end = lax.fori_loop(0, nc, step, tuple(init))
        for p in range(S5_PAIRS):
            car_r[p] = end[2 * p]
            car_i[p] = end[2 * p + 1]
        sr_ref[...] = car_r[...]
        si_ref[...] = car_i[...]
    else:
        for p in range(S5_PAIRS):
            sr, si = s0r_ref[p], s0i_ref[p]
            str_scr[p] = sr
            sti_scr[p] = si
            alr, ali = alr_ref[p], ali_ref[p]
            sr_ref[p] = alr * sr - ali * si + inr_scr[p]
            si_ref[p] = alr * si + ali * sr + ini_scr[p]

    for p in range(S5_PAIRS):
        sbr_scr[p] = str_scr[p].astype(BF16)
        sbi_scr[p] = sti_scr[p].astype(BF16)

    for q in range(S5_BLOCKS):
        for j in range(L):
            acc = _dot(xb_scr[j, q], mtd_ref[q, 0])
            for m in range(j):
                acc += _dot(xb_scr[m, q], mtd_ref[q, j - m])
            for r in range(ppb):
                p = q * ppb + r
                acc += _dot(sbr_scr[p], mgr_ref[p, j])
                acc += _dot(sbi_scr[p], mgi_ref[p, j])
            y = acc + dsk_ref[q] * x_block(j, q)
            y_refs[q][:, pl.ds(j, nc, stride=L), :] = y.reshape(nseq, nc, LANES)


def _s5_scan(u, s0_re, s0_im, mats, chained, tt):
    nseq, t, _ = u.shape
    n_states = s0_re.shape[0]
    L = S5_CHUNK
    nc = tt // L
    rows = nseq * nc
    assert rows == n_states or chained
    pairs = lambda s: s.reshape(n_states, S5_PAIRS, 2 * SSM_STATE).transpose(1, 0, 2)
    full = lambda a: pl.BlockSpec(a.shape, lambda i: (0,) * a.ndim)
    consts = [mats[k] for k in ('mtd', 'mf_r', 'mf_i', 'mg_r', 'mg_i', 'al_r', 'al_i', 'dsk')]
    state_shape = jax.ShapeDtypeStruct((S5_PAIRS, n_states, 2 * SSM_STATE), F32)
    state_spec = pl.BlockSpec((S5_PAIRS, n_states, 2 * SSM_STATE), lambda i: (0, 0, 0))
    pair_rows = lambda dt: pltpu.VMEM((S5_PAIRS, rows, LANES), dt)
    lane_block = lambda q: pl.BlockSpec((nseq, tt, LANES), lambda i: (0, i, q))
    out_block = pl.BlockSpec((nseq, tt, LANES), lambda i: (0, i, 0))
    *y, sr, si = pl.pallas_call(
        functools.partial(_s5_kernel, nseq=nseq, nc=nc, chained=chained),
        grid=(t // tt,),
        in_specs=[lane_block(q) for q in range(S5_BLOCKS)] + [state_spec, state_spec]
                 + [full(a) for a in consts],
        out_specs=[out_block] * S5_BLOCKS + [state_spec, state_spec],
        out_shape=[jax.ShapeDtypeStruct((nseq, t, LANES), F32)] * S5_BLOCKS + [state_shape, state_shape],
        scratch_shapes=[pltpu.VMEM((L, S5_BLOCKS, rows, LANES), BF16),
                        pair_rows(F32), pair_rows(F32), pair_rows(F32), pair_rows(F32),
                        pair_rows(BF16), pair_rows(BF16),
                        pltpu.VMEM((S5_PAIRS, n_states, 2 * SSM_STATE), F32),
                        pltpu.VMEM((S5_PAIRS, n_states, 2 * SSM_STATE), F32)],
        compiler_params=_cparams("arbitrary"),
        name="s5_scan",
    )(*([u] * S5_BLOCKS), pairs(s0_re), pairs(s0_im), *consts)
    unpair = lambda s: s.transpose(1, 0, 2).reshape(n_states, N_SSM_GROUPS, SSM_STATE)
    return y, unpair(sr), unpair(si)


KEY_NEG_INF = INT_MIN + 0x7FFFFF


def _key_to_float(x):
    return pltpu.bitcast(jnp.where(x < 0, x ^ jnp.int32(0x7FFFFFFF), x), F32)


def _threshold_of(x):
    return jnp.where(x <= KEY_NEG_INF, -jnp.inf, _key_to_float(x))


def _kth_largest(count_ge, shape, n_sel):
    def bit_step(i, x):
        cand = x + lax.shift_left(jnp.int32(1), 31 - i)
        return jnp.where(count_ge(_key_to_float(cand)) >= n_sel, cand, x)
    return _threshold_of(lax.fori_loop(0, 32, bit_step, jnp.full(shape, INT_MIN, jnp.int32)))


def _kth_largest_radix4(count_ge3, shape, n_sel):
    def step(i, x):
        one = lax.shift_left(jnp.int32(1), 30 - 2 * i)
        c1, c2, c3 = x + one, x + 2 * one, x + 3 * one
        n1, n2, n3 = count_ge3(_key_to_float(c1), _key_to_float(c2), _key_to_float(c3))
        return jnp.where(n3 >= n_sel, c3, jnp.where(n2 >= n_sel, c2, jnp.where(n1 >= n_sel, c1, x)))
    return _threshold_of(lax.fori_loop(0, 16, step, jnp.full(shape, INT_MIN, jnp.int32)))


def _rows_sum(x):
    r, c = x.shape
    return jnp.sum(jnp.sum(x.reshape(r // 8, 8, c), axis=0), axis=0, keepdims=True)


def _rows_max(x):
    r, c = x.shape
    return jnp.max(jnp.max(x.reshape(r // 8, 8, c), axis=0), axis=0, keepdims=True)


def _dsa_prompt_kernel(qt_ref, qit_ref, wit_ref, kb_ref, kib_ref, vt_ref, tri_ref, o_ref,
                       key_scr, bias_scr, acc_scr, *, tq, kbs, kba, n_sel):
    qtile = pl.program_id(1)
    n_keys = qtile * tq + tq
    nks = (n_keys + kbs - 1) // kbs
    nka = (n_keys + kba - 1) // kba

    def causal(j, kb):
        kpos = j * kb + lax.broadcasted_iota(jnp.int32, (kb, tq), 0)
        tpos = qtile * tq + lax.broadcasted_iota(jnp.int32, (kb, tq), 1)
        return kpos <= tpos

    qit = qit_ref[...]
    wi_w = jnp.concatenate([qit[h * IDX_DIM:(h + 1) * IDX_DIM] for h in range(N_IDX_HEADS)], axis=1)
    wit = wit_ref[...]

    def score_block(j, _):
        rows = pl.ds(pl.multiple_of(j * kbs, kbs), kbs)
        d = _dot(kib_ref[rows, :], wi_w)
        sc = jnp.zeros((kbs, tq), F32)
        for h in range(N_IDX_HEADS):
            sc += jnp.maximum(d[:, h * tq:(h + 1) * tq], 0.0) * wit[h:h + 1, :]
        key_scr[rows, :] = jnp.where(causal(j, kbs), sc, -jnp.inf)
        return 0
    lax.fori_loop(0, nka * (kba // kbs), score_block, 0)

    def count(pred):
        def body(j, acc):
            blk = key_scr[pl.ds(pl.multiple_of(j * kbs, kbs), kbs), :]
            hit = jnp.where(pred(blk), 1.0, 0.0)
            return acc + jnp.sum(hit.reshape(kbs // 32, 32, tq), axis=0)
        acc = lax.fori_loop(0, nks, body, jnp.zeros((32, tq), F32))
        return _rows_sum(acc)

    thr = _kth_largest(lambda cand: count(lambda blk: blk >= cand), (1, tq), n_sel)

    @pl.when(jnp.max(count(lambda blk: blk >= thr)) > n_sel)
    def _():
        room = n_sel - count(lambda blk: blk > thr)
        kbt = tri_ref.shape[0]

        def demote(j, tie_seen):
            rows = pl.ds(pl.multiple_of(j * kbt, kbt), kbt)
            key = key_scr[rows, :]
            tie = jnp.where(key == thr, 1.0, 0.0)
            rank = tie_seen + _dot(tri_ref[...], tie.astype(BF16))
            key_scr[rows, :] = jnp.where((tie > 0.0) & (rank > room), -jnp.inf, key)
            return tie_seen + _rows_sum(tie)
        lax.fori_loop(0, (n_keys + kbt - 1) // kbt, demote, jnp.zeros((1, tq), F32))

    qt = qt_ref[...]
    zero = jnp.zeros((HEAD_DIM, tq), BF16)
    w_qk = []
    for n in range(N_KV_HEADS):
        cols = []
        for g in range(2):
            h = 2 * n + g
            qh = qt[h * HEAD_DIM:(h + 1) * HEAD_DIM]
            cols.append(jnp.concatenate([qh, zero] if n % 2 == 0 else [zero, qh], axis=0))
        w_qk.append(jnp.concatenate(cols, axis=1))

    sub = min(128, kba)

    def scores(kblk, bias, n, i):
        r = slice(i * sub, (i + 1) * sub)
        s = _dot(kblk[r, (n // 2) * LANES:(n // 2 + 1) * LANES], w_qk[n])
        return s + jnp.concatenate([bias[r], bias[r]], axis=1)

    def max_block(j, carry):
        rows = pl.ds(pl.multiple_of(j * kba, kba), kba)
        bias = jnp.where((key_scr[rows, :] >= thr) & causal(j, kba), 0.0, NEG)
        bias_scr[rows, :] = bias
        kblk = kb_ref[rows, :]
        out = []
        for n in range(N_KV_HEADS):
            mx = carry[n]
            for i in range(kba // sub):
                s = scores(kblk, bias, n, i)
                mx = jnp.maximum(mx, jnp.max(s.reshape(sub // 8, 8, 2 * tq), axis=0))
            out.append(mx)
        return tuple(out)
    mx = lax.fori_loop(0, nka, max_block, (jnp.full((8, 2 * tq), NEG, F32),) * N_KV_HEADS)
    m_row = [jnp.max(m, axis=0, keepdims=True) for m in mx]

    acc_scr[...] = jnp.zeros(acc_scr.shape, F32)

    def attend_block(j, carry):
        rows = pl.ds(pl.multiple_of(j * kba, kba), kba)
        bias = bias_scr[rows, :]
        kblk = kb_ref[rows, :]
        out = []
        for n in range(N_KV_HEADS):
            ls = carry[n]
            pieces = []
            for i in range(kba // sub):
                p = jnp.exp(scores(kblk, bias, n, i) - m_row[n])
                ls = ls + jnp.sum(p.reshape(sub // 8, 8, 2 * tq), axis=0)
                pieces.append(p.astype(BF16))
            acc_scr[n] += _dot(vt_ref[n * HEAD_DIM:(n + 1) * HEAD_DIM, rows], jnp.concatenate(pieces, axis=0))
            out.append(ls)
        return tuple(out)
    ls = lax.fori_loop(0, nka, attend_block, (jnp.zeros((8, 2 * tq), F32),) * N_KV_HEADS)

    parts = []
    for n in range(N_KV_HEADS):
        o = acc_scr[n] / jnp.sum(ls[n], axis=0, keepdims=True)
        parts += [o[:, :tq], o[:, tq:]]
    o_ref[...] = jnp.concatenate(parts, axis=0).T


def _upper_tri(n):
    r = np.arange(n)
    return jnp.asarray(r[:, None] <= r[None, :], BF16)


def _lower_tri(n):
    r = np.arange(n)
    return jnp.asarray(r[:, None] >= r[None, :], BF16)


def _dsa_prompt(qt, qit, wit, kb, kib, vt, b, s_len, tq=256, kbs=256, kba=512, kbt=256):
    kbs, kba, kbt = min(kbs, s_len), min(kba, s_len), min(kbt, s_len)
    n_sel = min(TOPK_KEYS, s_len // 4)
    nq = s_len // tq
    qspec = lambda r: pl.BlockSpec((r, tq), lambda bi, qi_: (0, bi * nq + qi_))
    kspec = lambda w: pl.BlockSpec((s_len, w), lambda bi, qi_: (bi, 0))
    return pl.pallas_call(
        functools.partial(_dsa_prompt_kernel, tq=tq, kbs=kbs, kba=kba, n_sel=n_sel),
        grid=(b, nq),
        in_specs=[qspec(512), qspec(256), qspec(8), kspec(256), kspec(IDX_DIM),
                  pl.BlockSpec((256, s_len), lambda bi, qi_: (0, bi)),
                  pl.BlockSpec((kbt, kbt), lambda bi, qi_: (0, 0))],
        out_specs=pl.BlockSpec((tq, D_ATT), lambda bi, qi_: (bi * nq + qi_, 0)),
        out_shape=jax.ShapeDtypeStruct((b * s_len, D_ATT), F32),
        scratch_shapes=[pltpu.VMEM((s_len, tq), F32),
                        pltpu.VMEM((s_len, tq), F32),
                        pltpu.VMEM((N_KV_HEADS, HEAD_DIM, 2 * tq), F32)],
        compiler_params=_cparams("arbitrary", "arbitrary"),
        name="dsa_prompt",
    )(qt, qit, wit, kb, kib, vt, _lower_tri(kbt))


def _dsa_sample_kernel(pt_ref, q_ref, qis_ref, wi_ref, knew_ref, vnew_ref, kinew_ref, tri_ref,
                       ck_ref, cv_ref, cki_ref, o_ref,
                       kibuf, kbuf, vbuf, sc_scr, sem_ki, sem_k, sem_v,
                       *, n_pages, n_sel, kbi, kba, ts):
    b = pl.program_id(0)
    nb = pl.num_programs(0)
    slot = b % 2
    past = n_pages * PAGE_SIZE
    rows = N_HEADS * ts
    hrows = rows // N_KV_HEADS

    def page_keys(p):
        return pl.ds(pl.multiple_of(p * PAGE_SIZE, PAGE_SIZE), PAGE_SIZE)

    def ki_copy(bb, sl, p):
        return pltpu.make_async_copy(cki_ref.at[pt_ref[bb, p]], kibuf.at[sl, :, page_keys(p)], sem_ki.at[sl])

    def k_copy(bb, sl, p):
        return pltpu.make_async_copy(ck_ref.at[pt_ref[bb, p]], kbuf.at[sl, :, :, page_keys(p)], sem_k.at[sl])

    def v_copy(bb, sl, p):
        return pltpu.make_async_copy(cv_ref.at[pt_ref[bb, p]], vbuf.at[sl, :, :, page_keys(p)], sem_v.at[sl])

    def for_pages(fn):
        def body(p, _):
            fn(p)
            return 0
        lax.fori_loop(0, n_pages, body, 0)

    def start_all(bb, sl):
        def one(p):
            ki_copy(bb, sl, p).start()
            k_copy(bb, sl, p).start()
            v_copy(bb, sl, p).start()
        for_pages(one)

    @pl.when(b == 0)
    def _():
        start_all(b, slot)

    @pl.when(b + 1 < nb)
    def _():
        start_all(b + 1, 1 - slot)

    for_pages(lambda p: ki_copy(b, slot, p).wait())

    qis = qis_ref[...]
    wi = wi_ref[...]
    n_tiles = past // LANES + 1

    def index_scores(dots):
        d = jnp.maximum(dots, 0.0)
        sc = jnp.zeros((ts, dots.shape[1]), F32)
        for h in range(N_IDX_HEADS):
            sc += d[h * ts:(h + 1) * ts] * wi[:, h:h + 1]
        return sc

    def score_block(j, _):
        cols = pl.ds(pl.multiple_of(j * kbi, kbi), kbi)
        sc_scr[:, cols] = index_scores(_dot(qis, kibuf[slot, :, cols].astype(BF16)))
        return 0
    lax.fori_loop(0, past // kbi, score_block, 0)
    pad_rows = LANES - ts
    kin = jnp.concatenate([kinew_ref[...], jnp.zeros((pad_rows, IDX_DIM), F32)], axis=0)
    new_ok = (lax.broadcasted_iota(jnp.int32, (ts, LANES), 1)
              <= lax.broadcasted_iota(jnp.int32, (ts, LANES), 0))
    sc_scr[:, past:past + LANES] = jnp.where(new_ok, index_scores(_dot_nt(qis, kin.astype(BF16))), -jnp.inf)

    def counts(preds):
        sc = sc_scr[...]
        parts = [[jnp.zeros((ts, LANES), F32), jnp.zeros((ts, LANES), F32)] for _ in preds]
        for c in range(n_tiles):
            tile = sc[:, c * LANES:(c + 1) * LANES]
            for i, pred in enumerate(preds):
                parts[i][c % 2] = parts[i][c % 2] + jnp.where(pred(tile), 1.0, 0.0)
        return [jnp.sum(a + b_, axis=-1, keepdims=True) for a, b_ in parts]

    thr = _kth_largest_radix4(lambda c1, c2, c3: counts([lambda s: s >= c1, lambda s: s >= c2, lambda s: s >= c3]),
                              (ts, 1), n_sel)
    n_ge, n_gt = counts([lambda s: s >= thr, lambda s: s > thr])

    @pl.when(jnp.max(n_ge) > n_sel)
    def _():
        room = n_sel - n_gt
        kbt = tri_ref.shape[0]

        def demote(cols, tie_seen, tri):
            sc = sc_scr[:, cols]
            tie = jnp.where(sc == thr, 1.0, 0.0)
            rank = tie_seen + _dot(tie.astype(BF16), tri)
            sc_scr[:, cols] = jnp.where((tie > 0.0) & (rank > room), -jnp.inf, sc)
            return tie_seen + jnp.sum(tie, axis=-1, keepdims=True)
        seen = lax.fori_loop(0, past // kbt,
                             lambda j, s: demote(pl.ds(pl.multiple_of(j * kbt, kbt), kbt), s, tri_ref[...]),
                             jnp.zeros((ts, 1), F32))
        demote(pl.ds(past, LANES), seen, tri_ref[:LANES, :LANES])

    sc_scr[...] = jnp.where(sc_scr[...] >= thr, 0.0, NEG)
    sc_scr[:, past:past + LANES] = jnp.where(new_ok, sc_scr[:, past:past + LANES], NEG)

    for_pages(lambda p: k_copy(b, slot, p).wait())
    for_pages(lambda p: v_copy(b, slot, p).wait())

    q = q_ref[...]
    zpad = jnp.zeros((pad_rows, HEAD_DIM), F32)

    def key_block(j):
        return pl.ds(pl.multiple_of(j * kba, kba), kba)

    def head_rows(x, n):
        return x[n * hrows:(n + 1) * hrows]

    def scores(dots_of_head, bias):
        s = jnp.concatenate([dots_of_head(head_rows(q, n), n) for n in range(N_KV_HEADS)], axis=0)
        return s + jnp.concatenate([bias] * N_HEADS, axis=0)

    def block_scores(j):
        return scores(lambda qn, n: _dot(qn, kbuf[slot, n, :, key_block(j)].astype(BF16)), sc_scr[:, key_block(j)])

    def new_scores():
        return scores(lambda qn, n: _dot_nt(qn, jnp.concatenate([knew_ref[n], zpad], axis=0).astype(BF16)),
                      sc_scr[:, past:past + LANES])

    def lane_tiles(x, op, acc):
        for c in range(x.shape[1] // LANES):
            acc = op(acc, x[:, c * LANES:(c + 1) * LANES])
        return acc

    mx = lax.fori_loop(0, past // kba, lambda j, m: lane_tiles(block_scores(j), jnp.maximum, m),
                       jnp.full((rows, LANES), NEG, F32))
    m_row = jnp.max(jnp.maximum(mx, new_scores()), axis=-1, keepdims=True)

    def value_product(p, pv_of_head, accs):
        return tuple(accs[n] + pv_of_head(head_rows(p, n).astype(BF16), n) for n in range(N_KV_HEADS))

    def attend_block(j, carry):
        ls, accs = carry
        p = jnp.exp(block_scores(j) - m_row)
        accs = value_product(p, lambda pn, n: _dot_nt(pn, vbuf[slot, n, :, key_block(j)].astype(BF16)), accs)
        return lane_tiles(p, jnp.add, ls), accs
    zero_acc = (jnp.zeros((hrows, HEAD_DIM), F32),) * N_KV_HEADS
    ls, accs = lax.fori_loop(0, past // kba, attend_block, (jnp.zeros((rows, LANES), F32), zero_acc))
    p_new = jnp.exp(new_scores() - m_row)
    accs = value_product(
        p_new, lambda pn, n: _dot(pn, jnp.concatenate([vnew_ref[n], zpad], axis=0).astype(BF16)), accs)
    l_row = jnp.sum(ls + p_new, axis=-1, keepdims=True)
    o_ref[...] = jnp.concatenate(accs, axis=0) / l_row


def _dsa_sample(q, qi, wi, k, v, ki, cache_k, cache_v, cache_kidx, page_table, ts, kbi=2048, kba=1024, kbt=512):
    bd, n_pages = page_table.shape
    past = n_pages * PAGE_SIZE
    kbi, kba, kbt = min(kbi, past), min(kba, past), min(kbt, past)
    n_sel = min(TOPK_KEYS, (past + ts) // 4)
    rows = N_HEADS * ts
    heads_first = lambda a, nh: a.reshape(bd, ts, nh, -1).transpose(0, 2, 1, 3)
    qh = heads_first(q, N_HEADS).reshape(bd, rows, HEAD_DIM)
    qis = heads_first(qi, N_IDX_HEADS).reshape(bd, N_IDX_HEADS * ts, IDX_DIM)
    per_b = lambda *s: pl.BlockSpec((None,) + s, lambda i, pt: (i,) + (0,) * len(s))
    any_spec = pl.BlockSpec(memory_space=pl.ANY)
    page_buf = pltpu.VMEM((2, N_KV_HEADS, HEAD_DIM, past), F32)
    o = pl.pallas_call(
        functools.partial(_dsa_sample_kernel, n_pages=n_pages, n_sel=n_sel, kbi=kbi, kba=kba, ts=ts),
        grid_spec=pltpu.PrefetchScalarGridSpec(
            num_scalar_prefetch=1,
            grid=(bd,),
            in_specs=[per_b(rows, HEAD_DIM), per_b(N_IDX_HEADS * ts, IDX_DIM), per_b(ts, N_IDX_HEADS),
                      per_b(N_KV_HEADS, ts, HEAD_DIM), per_b(N_KV_HEADS, ts, HEAD_DIM), per_b(ts, IDX_DIM),
                      pl.BlockSpec((kbt, kbt), lambda i, pt: (0, 0)),
                      any_spec, any_spec, any_spec],
            out_specs=per_b(rows, HEAD_DIM),
            scratch_shapes=[pltpu.VMEM((2, IDX_DIM, past), F32), page_buf, page_buf,
                            pltpu.VMEM((ts, past + LANES), F32),
                            pltpu.SemaphoreType.DMA((2,)),
                            pltpu.SemaphoreType.DMA((2,)),
                            pltpu.SemaphoreType.DMA((2,))]),
        out_shape=jax.ShapeDtypeStruct((bd, rows, HEAD_DIM), F32),
        compiler_params=_cparams("arbitrary"),
        name="dsa_sample",
    )(page_table, qh, qis, wi.reshape(bd, ts, N_IDX_HEADS), heads_first(k, N_KV_HEADS),
      heads_first(v, N_KV_HEADS), ki.reshape(bd, ts, IDX_DIM), _upper_tri(kbt),
      cache_k.transpose(0, 2, 3, 1), cache_v.transpose(0, 2, 3, 1), cache_kidx.transpose(0, 2, 1))
    return o.reshape(bd, N_HEADS, ts, HEAD_DIM).transpose(0, 2, 1, 3).reshape(bd * ts, D_ATT)


def _layer_norm(x, g, b):
    mu = jnp.mean(x, axis=-1, keepdims=True)
    xc = x - mu
    var = jnp.mean(xc * xc, axis=-1, keepdims=True)
    return xc * lax.rsqrt(var + LN_EPS) * g + b


def _rms_norm(x, g):
    return x * lax.rsqrt(jnp.mean(x * x, axis=-1, keepdims=True) + LN_EPS) * g


def _mix_kernel(*refs):
    ys_refs, refs = refs[:S5_BLOCKS], refs[S5_BLOCKS:]
    (ya_ref, x_ref, g1_ref, sc2_ref, sh2_ref, wglu_ref, bglu_ref, gs_ref, ga_ref,
     wout_ref, l1g_ref, l1b_ref, wr_ref, br_ref, tri_ref,
     x1_ref, h2_ref, eidx_ref, gate_ref, rank_ref, cnt_ref, run_scr) = refs
    g = jax.nn.gelu(jnp.concatenate([r[...] for r in ys_refs], axis=-1))
    ssm = g * jax.nn.sigmoid(_dot(g.astype(BF16), wglu_ref[...]) + bglu_ref[...])
    cat = jnp.concatenate([_rms_norm(ssm, gs_ref[...]), _rms_norm(ya_ref[...], ga_ref[...])], axis=-1)
    mixed = _dot(cat.astype(BF16), wout_ref[...])
    x1 = _layer_norm(ALPHA * x_ref[...] + g1_ref[...] * mixed, l1g_ref[...], l1b_ref[...])
    x1_ref[...] = x1
    h2f = x1 * (1.0 + sc2_ref[...]) + sh2_ref[...]
    h2_ref[...] = h2f
    h2 = h2f.astype(BF16)
    aff = jax.nn.sigmoid(_dot(h2, wr_ref[...]))
    vals = aff + br_ref[...]
    tm = aff.shape[0]
    lane = lax.broadcasted_iota(jnp.int32, (tm, N_EXPERTS), 1)
    col = lax.broadcasted_iota(jnp.int32, (tm, 8), 1)
    eidx = jnp.zeros((tm, 8), jnp.int32)
    gsel = jnp.zeros((tm, 8), F32)
    chosen = jnp.zeros((tm, N_EXPERTS), F32)
    picks = []
    for j in range(TOP_K_EXPERTS):
        top = jnp.max(vals, axis=-1, keepdims=True)
        idx = jnp.min(jnp.where(vals == top, lane, N_EXPERTS), axis=-1, keepdims=True)
        hit = lane == idx
        gj = jnp.sum(jnp.where(hit, aff, 0.0), axis=-1, keepdims=True)
        eidx = jnp.where(col == j, idx, eidx)
        gsel = jnp.where(col == j, gj, gsel)
        vals = jnp.where(hit, -jnp.inf, vals)
        chosen = jnp.where(hit, 1.0, chosen)
        picks.append(idx)
    eidx_ref[...] = eidx
    gate_ref[...] = gsel / jnp.sum(gsel, axis=-1, keepdims=True) * ROUTED_SCALE

    @pl.when(pl.program_id(0) == 0)
    def _():
        run_scr[...] = jnp.zeros(run_scr.shape, F32)
    before = run_scr[...] + _dot(tri_ref[...], chosen.astype(BF16))
    rank = jnp.zeros((tm, 8), jnp.int32)
    for j in range(TOP_K_EXPERTS):
        rj = jnp.sum(jnp.where(lane == picks[j], before, 0.0), axis=-1, keepdims=True)
        rank = jnp.where(col == j, rj.astype(jnp.int32), rank)
    rank_ref[...] = rank
    run_scr[...] += jnp.sum(chosen, axis=0, keepdims=True)
    cnt_ref[...] = jnp.broadcast_to(run_scr[...], cnt_ref.shape)


def _mixer_out(ys, ya, x, g1, sc2, sh2, w, rows_per_mod, tm):
    t = x.shape[0]
    ms = _mod_spec(g1, tm, rows_per_mod)
    row = lambda wd: pl.BlockSpec((tm, wd), lambda i: (i, 0))
    full = lambda a: pl.BlockSpec(a.shape, lambda i: (0, 0))
    r = np.arange(tm)
    strict_lower = jnp.asarray(r[None, :] < r[:, None], BF16)
    consts = [w['w_glu'], w['b_glu'], w['g_ssm_out'], w['g_att_out'], w['w_out'], w['ln1_g'], w['ln1_b'],
              w['w_router'], w['b_router'], strict_lower]
    return pl.pallas_call(
        _mix_kernel,
        grid=(t // tm,),
        in_specs=[row(LANES)] * S5_BLOCKS + [row(D_ATT), row(D_MODEL), ms, ms, ms]
                 + [full(a) for a in consts],
        out_specs=[row(D_MODEL), row(D_MODEL), row(8), row(8), row(8),
                   pl.BlockSpec((8, N_EXPERTS), lambda i: (0, 0))],
        out_shape=[jax.ShapeDtypeStruct((t, D_MODEL), F32), jax.ShapeDtypeStruct((t, D_MODEL), F32),
                   jax.ShapeDtypeStruct((t, 8), jnp.int32), jax.ShapeDtypeStruct((t, 8), F32),
                   jax.ShapeDtypeStruct((t, 8), jnp.int32), jax.ShapeDtypeStruct((8, N_EXPERTS), F32)],
        scratch_shapes=[pltpu.VMEM((1, N_EXPERTS), F32)],
        compiler_params=_cparams("arbitrary"),
        name="mixer_out",
    )(*ys, ya, x, g1, sc2, sh2, *consts)


MOE_ROWS = 512


def _experts_kernel(be_ref, nused_ref, x_ref, w1_ref, w3_ref, w2_ref, y_ref):
    i = pl.program_id(0)

    @pl.when(i < nused_ref[0])
    def _():
        x = x_ref[...].astype(BF16)
        h1 = _dot(x, w1_ref[...].astype(BF16))
        h3 = _dot(x, w3_ref[...].astype(BF16))
        a = (h1 * jax.nn.sigmoid(h1) * h3).astype(BF16)
        y_ref[...] = _dot(a, w2_ref[...].astype(BF16)).astype(y_ref.dtype)

    @pl.when(i >= nused_ref[0])
    def _():
        y_ref[...] = jnp.zeros(y_ref.shape, y_ref.dtype)


def _experts(xs, block_e, n_used, w_e1, w_e3, w_e2):
    n_slots = xs.shape[0]
    n_blocks = n_slots // MOE_ROWS
    wspec = lambda r, c: pl.BlockSpec((None, r, c), lambda i, be, nu: (be[i], 0, 0))
    return pl.pallas_call(
        _experts_kernel,
        grid_spec=pltpu.PrefetchScalarGridSpec(
            num_scalar_prefetch=2,
            grid=(n_blocks,),
            in_specs=[pl.BlockSpec((MOE_ROWS, D_MODEL), lambda i, be, nu: (i, 0)),
                      wspec(D_MODEL, D_EXPERT), wspec(D_MODEL, D_EXPERT), wspec(D_EXPERT, D_MODEL)],
            out_specs=pl.BlockSpec((MOE_ROWS, D_MODEL), lambda i, be, nu: (i, 0))),
        out_shape=jax.ShapeDtypeStruct((n_slots, D_MODEL), F32),
        compiler_params=_cparams("arbitrary"),
        name="routed_experts",
    )(block_e, n_used, xs, w_e1, w_e3, w_e2)


def _dispatch_plan(eidx, rank, counts, n_tok):
    padded = (counts + MOE_ROWS - 1) // MOE_ROWS * MOE_ROWS
    pends = jnp.cumsum(padded)
    pstarts = pends - padded
    slot = pstarts[eidx] + rank
    n_blocks = -(-n_tok * TOP_K_EXPERTS // MOE_ROWS) + N_EXPERTS
    first_row = jnp.arange(n_blocks, dtype=jnp.int32) * MOE_ROWS
    block_e = jnp.sum((pends[None, :] <= first_row[:, None]).astype(jnp.int32), axis=1)
    block_e = jnp.minimum(block_e, N_EXPERTS - 1)
    n_used = (pends[-1:] // MOE_ROWS).astype(jnp.int32)
    return slot, block_e, n_used, n_blocks * MOE_ROWS


def _final_kernel(*refs):
    ye_refs = refs[:TOP_K_EXPERTS]
    gate_ref, h2_ref, x1_ref, g2_ref, ws1_ref, ws3_ref, ws2_ref, l2g_ref, l2b_ref, o_ref = refs[TOP_K_EXPERTS:]
    h2 = h2_ref[...].astype(BF16)
    h1 = _dot(h2, ws1_ref[...])
    h3 = _dot(h2, ws3_ref[...])
    a = (h1 * jax.nn.sigmoid(h1) * h3).astype(BF16)
    gate = gate_ref[...]
    routed = ye_refs[0][...].astype(F32) * gate[:, 0:1]
    for j in range(1, TOP_K_EXPERTS):
        routed += ye_refs[j][...].astype(F32) * gate[:, j:j + 1]
    ffn = routed + _dot(a, ws2_ref[...])
    o_ref[...] = _layer_norm(ALPHA * x1_ref[...] + g2_ref[...] * ffn, l2g_ref[...], l2b_ref[...])


def _final(ye_parts, gates, h2, x1, g2, w, rows_per_mod, tm):
    t = x1.shape[0]
    row = lambda wd: pl.BlockSpec((tm, wd), lambda i: (i, 0))
    full = lambda a: pl.BlockSpec(a.shape, lambda i: (0, 0))
    consts = [w['w_s1'], w['w_s3'], w['w_s2'], w['ln2_g'], w['ln2_b']]
    return pl.pallas_call(
        _final_kernel,
        grid=(t // tm,),
        in_specs=[row(D_MODEL)] * TOP_K_EXPERTS
                 + [row(8), row(D_MODEL), row(D_MODEL), _mod_spec(g2, tm, rows_per_mod)]
                 + [full(a) for a in consts],
        out_specs=row(D_MODEL),
        out_shape=jax.ShapeDtypeStruct((t, D_MODEL), F32),
        compiler_params=_cparams("arbitrary"),
        name="final_residual",
    )(*ye_parts, gates, h2, x1, g2, *consts)


def kernel(x_prompt, x_sample, c_prompt, c_sample, cache_k, cache_v, cache_kidx, state_ssm_re, state_ssm_im, page_table, w_ada, b_ada, w_in, ssm_a_re, ssm_a_im, ssm_log_dt, ssm_b_re, ssm_b_im, ssm_c_re, ssm_c_im, ssm_d, w_glu, b_glu, g_ssm_out, g_att_out, w_out, ln1_g, ln1_b, w_router, b_router, w_e1, w_e3, w_e2, w_s1, w_s3, w_s2, ln2_g, ln2_b):
    assert w_in.shape[0] == 1, "one layer"
    b, s, d = x_prompt.shape
    bd, ts, _ = x_sample.shape
    tp, tsn = b * s, bd * ts
    tm = 512
    row = lambda a: a[0].reshape(1, -1)
    w = dict(w_glu=w_glu[0].astype(BF16), b_glu=row(b_glu), g_ssm_out=row(g_ssm_out), g_att_out=row(g_att_out),
             w_out=w_out[0].astype(BF16), ln1_g=row(ln1_g), ln1_b=row(ln1_b),
             w_router=w_router[0].astype(BF16), b_router=row(b_router),
             w_s1=w_s1[0].astype(BF16), w_s3=w_s3[0].astype(BF16), w_s2=w_s2[0].astype(BF16),
             ln2_g=row(ln2_g), ln2_b=row(ln2_b))
    w_sample, w_prompt, w_prompt_t = _projection_weights(w_in[0])
    ssm_args = (ssm_a_re[0], ssm_a_im[0], ssm_log_dt[0], ssm_b_re[0], ssm_b_im[0], ssm_c_re[0], ssm_c_im[0],
                ssm_d[0])

    mod = _ada_mod(jnp.concatenate([c_prompt, c_sample], axis=0), w_ada[0], b_ada[0])
    mod_p = mod[:b].reshape(b, 6, 1, d)
    mod_s = jnp.broadcast_to(mod[b:].reshape(bd, 1, 6, d), (bd, ts, 6, d)).reshape(tsn, 6, d)

    xp = x_prompt.reshape(tp, d)
    u, kb, kib, qt, qit, vt, wit, kt_p, vt_p, kit_p = _in_projection_prompt(
        xp, mod_p[:, 1], mod_p[:, 0], w_prompt, w_prompt_t, s, tm)
    mats = _ssm_matrices(*ssm_args)
    zero_state = jnp.zeros((b, N_SSM_GROUPS, SSM_STATE), F32)
    ys_p, re_p, im_p = _s5_scan(u.reshape(b, s, D_SSM), zero_state, zero_state, mats, True, min(s, 128))
    ya_p = _dsa_prompt(qt, qit, wit, kb, kib, vt, b, s)
    x1_p, h2_p, eidx_p, gate_p, rank_p, cnt_p = _mixer_out([y.reshape(tp, LANES) for y in ys_p], ya_p, xp,
                                                           mod_p[:, 2], mod_p[:, 4], mod_p[:, 3], w, s, tm)

    assert ts == S5_CHUNK, "each sample sequence is one S5 chunk"
    xs = x_sample.reshape(tsn, d)
    tms = min(tm, tsn)
    u, q, k_s, v_s, qi, ki_s, wi = _in_projection(xs, mod_s[:, 1], mod_s[:, 0], w_sample, None, tms)
    ys_s, re_s, im_s = _s5_scan(u.reshape(1, tsn, D_SSM), state_ssm_re[0], state_ssm_im[0], mats, False, tsn)
    ya_s = _dsa_sample(q, qi, wi, k_s, v_s, ki_s, cache_k[0], cache_v[0], cache_kidx[0], page_table, ts)
    x1_s, h2_s, eidx_s, gate_s, rank_s, cnt_s = _mixer_out([y.reshape(tsn, LANES) for y in ys_s], ya_s, xs,
                                                           mod_s[:, 2], mod_s[:, 4], mod_s[:, 3], w, None, tms)

    n_tok = tp + tsn
    cnt_p, cnt_s = cnt_p[0].astype(jnp.int32), cnt_s[0].astype(jnp.int32)
    eidx_p, eidx_s = eidx_p[:, :TOP_K_EXPERTS], eidx_s[:, :TOP_K_EXPERTS]
    eidx = jnp.concatenate([eidx_p, eidx_s], axis=0)
    rank = jnp.concatenate([rank_p[:, :TOP_K_EXPERTS], rank_s[:, :TOP_K_EXPERTS] + cnt_p[eidx_s]], axis=0)
    slot, block_e, n_used, n_slots = _dispatch_plan(eidx, rank, cnt_p + cnt_s, n_tok)
    tok = jnp.broadcast_to(jnp.arange(n_tok, dtype=jnp.int32)[:, None], slot.shape)
    slot_tok = jnp.zeros((n_slots,), jnp.int32).at[slot.reshape(-1)].set(tok.reshape(-1))
    h2 = jnp.concatenate([h2_p, h2_s], axis=0)
    ye = _experts(h2[slot_tok], block_e, n_used, w_e1[0], w_e3[0], w_e2[0])
    parts_p = [ye[slot[:tp, j]] for j in range(TOP_K_EXPERTS)]
    parts_s = [ye[slot[tp:, j]] for j in range(TOP_K_EXPERTS)]

    y_p = _final(parts_p, gate_p, h2_p, x1_p, mod_p[:, 5], w, s, tm)
    y_s = _final(parts_s, gate_s, h2_s, x1_s, mod_s[:, 5], w, None, tms)

    heads_last = lambda a: a.reshape(b, N_KV_HEADS, HEAD_DIM, s).transpose(0, 3, 1, 2)[None]
    kvd = (1, bd, ts, N_KV_HEADS, HEAD_DIM)
    return (y_p.reshape(b, s, d), y_s.reshape(bd, ts, d),
            heads_last(kt_p), heads_last(vt_p), kit_p.transpose(0, 2, 1)[None], re_p[None], im_p[None],
            k_s.reshape(kvd), v_s.reshape(kvd), ki_s.reshape(1, bd, ts, IDX_DIM), re_s[None], im_s[None])
```

```python
import functools
import math

import jax
import jax.numpy as jnp
import numpy as np
from jax import lax
from jax.experimental import pallas as pl
from jax.experimental.pallas import tpu as pltpu

D_MODEL = 1024
D_SSM = 512
SSM_GROUP = 16
N_SSM_GROUPS = 32
SSM_STATE = 64
D_ATT = 512
HEAD_DIM = 64
N_HEADS = 8
N_KV_HEADS = 4
N_IDX_HEADS = 4
IDX_DIM = 64
TOPK_KEYS = 256
N_EXPERTS = 64
TOP_K_EXPERTS = 6
D_EXPERT = 256
ROUTED_SCALE = 2.5
PAGE_SIZE = 128
ALPHA = 2.0 ** 0.25
LN_EPS = 1e-5

LANES = 128
VMEM_LIMIT = 56 * 1024 * 1024
NEG = -1e30
INT_MIN = -2 ** 31

F32 = jnp.float32
BF16 = jnp.bfloat16
HI = lax.Precision.HIGHEST


def _cparams(*sem):
    return pltpu.CompilerParams(dimension_semantics=sem, vmem_limit_bytes=VMEM_LIMIT)


def _dot(a, b):
    return jnp.dot(a, b, preferred_element_type=F32)


def _dot_nt(a, b):
    return lax.dot_general(a, b, (((1,), (1,)), ((), ())), preferred_element_type=F32)


def _ada_kernel(c_ref, w_ref, b_ref, o_ref):
    c = c_ref[...]
    s = (c * jax.nn.sigmoid(c)).astype(BF16)
    o_ref[...] = _dot(s, w_ref[...].astype(BF16)) + b_ref[...]


def _ada_mod(c, w_ada, b_ada):
    n, d = c.shape
    dn = w_ada.shape[1]
    tn = 1024
    return pl.pallas_call(
        _ada_kernel,
        grid=(dn // tn,),
        in_specs=[pl.BlockSpec((n, d), lambda j: (0, 0)),
                  pl.BlockSpec((d, tn), lambda j: (0, j)),
                  pl.BlockSpec((1, tn), lambda j: (0, j))],
        out_specs=pl.BlockSpec((n, tn), lambda j: (0, j)),
        out_shape=jax.ShapeDtypeStruct((n, dn), F32),
        compiler_params=_cparams("arbitrary"),
        name="ada_mod",
    )(c, w_ada, b_ada.reshape(1, dn))


_C_U, _C_Q, _C_K, _C_V, _C_QI, _C_KI, _C_WI, _C_END = 0, 512, 1024, 1280, 1536, 1792, 1920, 2048
_P_U, _P_K, _P_KI, _P_END = 0, 512, 768, 896
_T_Q, _T_QI, _T_V, _T_WI, _T_K, _T_KI, _T_END = 0, 512, 768, 1024, 1032, 1288, 1352


def _projection_weights(w_in):
    d = w_in.shape[0]
    cuts = np.cumsum([D_SSM, N_HEADS * HEAD_DIM, N_KV_HEADS * HEAD_DIM, N_KV_HEADS * HEAD_DIM,
                      N_IDX_HEADS * IDX_DIM, IDX_DIM])
    wu, wq, wk, wv, wqi, wki, wwi = jnp.split(w_in, cuts.tolist(), axis=1)
    wq, wqi, wwi = wq * HEAD_DIM ** -0.5, wqi * IDX_DIM ** -0.5, wwi * N_IDX_HEADS ** -0.5
    z = lambda n: jnp.zeros((d, n), w_in.dtype)
    sample = jnp.concatenate([wu, wq, wk, wv, wqi, wki, z(64), wwi, z(124)], axis=1)
    prompt = jnp.concatenate([wu, wk, wki, z(64)], axis=1)
    prompt_t = jnp.concatenate([wq, wqi, wv, wwi, z(4), wk, wki], axis=1).T
    return sample.astype(BF16), prompt.astype(BF16), prompt_t.astype(BF16)


def _inproj_kernel(x_ref, sc_ref, sh_ref, w_ref, u_ref, q_ref, k_ref, v_ref, qi_ref, ki_ref, wi_ref):
    h = (x_ref[...] * (1.0 + sc_ref[...]) + sh_ref[...]).astype(BF16)
    r = _dot(h, w_ref[...])
    u_ref[...] = r[:, _C_U:_C_Q]
    q_ref[...] = r[:, _C_Q:_C_K].astype(BF16)
    k_ref[...] = r[:, _C_K:_C_V]
    v_ref[...] = r[:, _C_V:_C_QI]
    qi_ref[...] = r[:, _C_QI:_C_KI].astype(BF16)
    ki_ref[...] = r[:, _C_KI:_C_KI + IDX_DIM]
    wi_ref[...] = r[:, _C_WI:_C_WI + N_IDX_HEADS]


def _inproj_prompt_kernel(x_ref, sc_ref, sh_ref, w_ref, wt_ref,
                          u_ref, kb_ref, kib_ref, qt_ref, qit_ref, vtb_ref, wit_ref, kt_ref, vt_ref, kit_ref):
    h = (x_ref[...] * (1.0 + sc_ref[...]) + sh_ref[...]).astype(BF16)
    r = _dot(h, w_ref[...])
    u_ref[...] = r[:, _P_U:_P_K]
    kb_ref[...] = r[:, _P_K:_P_KI].astype(BF16)
    kib_ref[...] = r[:, _P_KI:_P_KI + IDX_DIM].astype(BF16)
    rt = _dot_nt(wt_ref[...], h)
    qt_ref[...] = rt[_T_Q:_T_QI].astype(BF16)
    qit_ref[...] = rt[_T_QI:_T_V].astype(BF16)
    vt = rt[_T_V:_T_WI]
    vtb_ref[...] = vt.astype(BF16)
    wit_ref[...] = rt[_T_WI:_T_K]
    kt_ref[...] = rt[_T_K:_T_KI]
    vt_ref[...] = vt
    kit_ref[...] = rt[_T_KI:_T_END]


def _mod_spec(mod, tm, rows_per_mod):
    if rows_per_mod is None:
        return pl.BlockSpec((tm, D_MODEL), lambda i: (i, 0))
    tiles = rows_per_mod // tm
    return pl.BlockSpec((None, 1, D_MODEL), lambda i: (i // tiles, 0, 0))


def _in_projection(x, sc, sh, w_tok, rows_per_mod, tm):
    t = x.shape[0]
    widths = [(D_SSM, F32), (512, BF16), (256, F32), (256, F32), (256, BF16), (IDX_DIM, F32),
              (N_IDX_HEADS, F32)]
    ms = _mod_spec(sc, tm, rows_per_mod)
    return pl.pallas_call(
        _inproj_kernel,
        grid=(t // tm,),
        in_specs=[pl.BlockSpec((tm, D_MODEL), lambda i: (i, 0)), ms, ms,
                  pl.BlockSpec(w_tok.shape, lambda i: (0, 0))],
        out_specs=[pl.BlockSpec((tm, w), lambda i: (i, 0)) for w, _ in widths],
        out_shape=[jax.ShapeDtypeStruct((t, w), dt) for w, dt in widths],
        compiler_params=_cparams("arbitrary"),
        name="in_projection",
    )(x, sc, sh, w_tok)


def _in_projection_prompt(x, sc, sh, w_tok, w_feat, rows_per_mod, tm):
    t = x.shape[0]
    tiles = rows_per_mod // tm
    nb = t // rows_per_mod
    tok = [(D_SSM, F32), (256, BF16), (IDX_DIM, BF16)]
    feat = [(512, BF16), (256, BF16), (256, BF16), (8, F32)]
    seq = [256, 256, IDX_DIM]
    ms = _mod_spec(sc, tm, rows_per_mod)
    return pl.pallas_call(
        _inproj_prompt_kernel,
        grid=(t // tm,),
        in_specs=[pl.BlockSpec((tm, D_MODEL), lambda i: (i, 0)), ms, ms,
                  pl.BlockSpec(w_tok.shape, lambda i: (0, 0)), pl.BlockSpec(w_feat.shape, lambda i: (0, 0))],
        out_specs=[pl.BlockSpec((tm, w), lambda i: (i, 0)) for w, _ in tok]
                  + [pl.BlockSpec((r, tm), lambda i: (0, i)) for r, _ in feat]
                  + [pl.BlockSpec((None, r, tm), lambda i: (i // tiles, 0, i % tiles)) for r in seq],
        out_shape=[jax.ShapeDtypeStruct((t, w), dt) for w, dt in tok]
                  + [jax.ShapeDtypeStruct((r, t), dt) for r, dt in feat]
                  + [jax.ShapeDtypeStruct((nb, r, rows_per_mod), F32) for r in seq],
        compiler_params=_cparams("arbitrary"),
        name="in_projection_prompt",
    )(x, sc, sh, w_tok, w_feat)


S5_CHUNK = 8
S5_BLOCKS = D_SSM // LANES
S5_GROUPS_PER_BLOCK = LANES // SSM_GROUP
S5_PAIRS = N_SSM_GROUPS // 2


def _ssm_matrices(a_re, a_im, log_dt, b_re, b_im, c_re, c_im, d_skip):
    g, p = a_re.shape
    L = S5_CHUNK
    dt = jnp.exp(log_dt)[:, None]
    mag = jnp.exp(dt * a_re)
    abr, abi = mag * jnp.cos(dt * a_im), mag * jnp.sin(dt * a_im)
    den = a_re * a_re + a_im * a_im
    nr, ni = abr - 1.0, abi
    fr = (nr * a_re + ni * a_im) / den
    fi = (ni * a_re - nr * a_im) / den
    bbr = fr[..., None] * b_re - fi[..., None] * b_im
    bbi = fr[..., None] * b_im + fi[..., None] * b_re
    dd = jnp.arange(L + 1, dtype=F32)[:, None, None]
    pm = jnp.exp(dd * (dt * a_re)[None])
    pw_r = pm * jnp.cos(dd * (dt * a_im)[None])
    pw_i = pm * jnp.sin(dd * (dt * a_im)[None])
    ca_r = c_re[None] * pw_r[:, :, None, :] - c_im[None] * pw_i[:, :, None, :]
    ca_i = c_re[None] * pw_i[:, :, None, :] + c_im[None] * pw_r[:, :, None, :]
    kd = (jnp.einsum('dgop,gpi->dgoi', ca_r[:L], bbr, precision=HI)
          - jnp.einsum('dgop,gpi->dgoi', ca_i[:L], bbi, precision=HI))
    nblk, gblk, npair = S5_BLOCKS, S5_GROUPS_PER_BLOCK, S5_PAIRS
    ppb = npair // nblk
    kd_t = jnp.transpose(kd, (1, 0, 3, 2)).reshape(nblk, gblk, L, SSM_GROUP, SSM_GROUP)
    mtd = jnp.einsum('qgdio,gh->qdgiho', kd_t, jnp.eye(gblk, dtype=F32))
    mtd = mtd.reshape(nblk, L, LANES, LANES)
    sel = np.zeros((ppb, gblk, 2), np.float32)
    for r in range(ppb):
        for t in range(2):
            sel[r, 2 * r + t, t] = 1.0
    sel = jnp.asarray(sel)
    rev_r, rev_i = pw_r[:L][::-1], pw_i[:L][::-1]
    f_r = rev_r[..., None] * bbr[None] - rev_i[..., None] * bbi[None]
    f_i = rev_r[..., None] * bbi[None] + rev_i[..., None] * bbr[None]

    def place_in(f):
        f = f.reshape(L, nblk, ppb, 2, p, SSM_GROUP)
        f = jnp.einsum('mqrtPi,rgt->qrmgitP', f, sel)
        return f.reshape(npair, L, LANES, 2 * p).astype(BF16)

    def place_out(c):
        c = c.reshape(L, nblk, ppb, 2, SSM_GROUP, p)
        c = jnp.einsum('jqrtoP,rgt->qrjtPgo', c, sel)
        return c.reshape(npair, L, 2 * p, LANES).astype(BF16)

    return dict(mtd=mtd.astype(BF16), mf_r=place_in(f_r), mf_i=place_in(f_i),
                mg_r=place_out(ca_r[1:]), mg_i=place_out(-ca_i[1:]),
                al_r=pw_r[L].reshape(npair, 1, 2 * p), al_i=pw_i[L].reshape(npair, 1, 2 * p),
                dsk=d_skip.reshape(nblk, 1, LANES))


def _s5_kernel(*refs, nseq, nc, chained):
    u_refs, refs = refs[:S5_BLOCKS], refs[S5_BLOCKS:]
    (s0r_ref, s0i_ref, mtd_ref, mfr_ref, mfi_ref, mgr_ref, mgi_ref, alr_ref, ali_ref, dsk_ref), refs = \
        refs[:10], refs[10:]
    y_refs, refs = refs[:S5_BLOCKS], refs[S5_BLOCKS:]
    sr_ref, si_ref, xb_scr, inr_scr, ini_scr, str_scr, sti_scr, sbr_scr, sbi_scr, car_r, car_i = refs
    L = S5_CHUNK
    rows = nseq * nc
    ppb = S5_PAIRS // S5_BLOCKS

    def x_block(m, q):
        return u_refs[q][:, pl.ds(m, nc, stride=L), :].reshape(rows, LANES)

    for m in range(L):
        for q in range(S5_BLOCKS):
            xb_scr[m, q] = x_block(m, q).astype(BF16)

    for p in range(S5_PAIRS):
        q = p // ppb
        acc_r = _dot(xb_scr[0, q], mfr_ref[p, 0])
        acc_i = _dot(xb_scr[0, q], mfi_ref[p, 0])
        for m in range(1, L):
            acc_r += _dot(xb_scr[m, q], mfr_ref[p, m])
            acc_i += _dot(xb_scr[m, q], mfi_ref[p, m])
        inr_scr[p] = acc_r
        ini_scr[p] = acc_i

    if chained:
        @pl.when(pl.program_id(0) == 0)
        def _():
            car_r[...] = s0r_ref[...]
            car_i[...] = s0i_ref[...]

        def step(c, carry):
            idx = pl.ds(c, nseq, stride=nc)
            out = []
            for p in range(S5_PAIRS):
                sr, si = carry[2 * p], carry[2 * p + 1]
                str_scr[p, idx, :] = sr
                sti_scr[p, idx, :] = si
                alr, ali = alr_ref[p], ali_ref[p]
                out.append(alr * sr - ali * si + inr_scr[p, idx, :])
                out.append(alr * si + ali * sr + ini_scr[p, idx, :])
            return tuple(out)

        init = []
        for p in range(S5_PAIRS):
            init += [car_r[p], car_i[p]]
        end = lax.fori_loop(0, nc, step, tuple(init))
        for p in range(S5_PAIRS):
            car_r[p] = end[2 * p]
            car_i[p] = end[2 * p + 1]
        sr_ref[...] = car_r[...]
        si_ref[...] = car_i[...]
    else:
        for p in range(S5_PAIRS):
            sr, si = s0r_ref[p], s0i_ref[p]
            str_scr[p] = sr
            sti_scr[p] = si
            alr, ali = alr_ref[p], ali_ref[p]
            sr_ref[p] = alr * sr - ali * si + inr_scr[p]
            si_ref[p] = alr * si + ali * sr + ini_scr[p]

    for p in range(S5_PAIRS):
        sbr_scr[p] = str_scr[p].astype(BF16)
        sbi_scr[p] = sti_scr[p].astype(BF16)

    for q in range(S5_BLOCKS):
        for j in range(L):
            acc = _dot(xb_scr[j, q], mtd_ref[q, 0])
            for m in range(j):
                acc += _dot(xb_scr[m, q], mtd_ref[q, j - m])
            for r in range(ppb):
                p = q * ppb + r
                acc += _dot(sbr_scr[p], mgr_ref[p, j])
                acc += _dot(sbi_scr[p], mgi_ref[p, j])
            y = acc + dsk_ref[q] * x_block(j, q)
            y_refs[q][:, pl.ds(j, nc, stride=L), :] = y.reshape(nseq, nc, LANES)


def _s5_scan(u, s0_re, s0_im, mats, chained, tt):
    nseq, t, _ = u.shape
    n_states = s0_re.shape[0]
    L = S5_CHUNK
    nc = tt // L
    rows = nseq * nc
    assert rows == n_states or chained
    pairs = lambda s: s.reshape(n_states, S5_PAIRS, 2 * SSM_STATE).transpose(1, 0, 2)
    full = lambda a: pl.BlockSpec(a.shape, lambda i: (0,) * a.ndim)
    consts = [mats[k] for k in ('mtd', 'mf_r', 'mf_i', 'mg_r', 'mg_i', 'al_r', 'al_i', 'dsk')]
    state_shape = jax.ShapeDtypeStruct((S5_PAIRS, n_states, 2 * SSM_STATE), F32)
    state_spec = pl.BlockSpec((S5_PAIRS, n_states, 2 * SSM_STATE), lambda i: (0, 0, 0))
    pair_rows = lambda dt: pltpu.VMEM((S5_PAIRS, rows, LANES), dt)
    lane_block = lambda q: pl.BlockSpec((nseq, tt, LANES), lambda i: (0, i, q))
    out_block = pl.BlockSpec((nseq, tt, LANES), lambda i: (0, i, 0))
    *y, sr, si = pl.pallas_call(
        functools.partial(_s5_kernel, nseq=nseq, nc=nc, chained=chained),
        grid=(t // tt,),
        in_specs=[lane_block(q) for q in range(S5_BLOCKS)] + [state_spec, state_spec]
                 + [full(a) for a in consts],
        out_specs=[out_block] * S5_BLOCKS + [state_spec, state_spec],
        out_shape=[jax.ShapeDtypeStruct((nseq, t, LANES), F32)] * S5_BLOCKS + [state_shape, state_shape],
        scratch_shapes=[pltpu.VMEM((L, S5_BLOCKS, rows, LANES), BF16),
                        pair_rows(F32), pair_rows(F32), pair_rows(F32), pair_rows(F32),
                        pair_rows(BF16), pair_rows(BF16),
                        pltpu.VMEM((S5_PAIRS, n_states, 2 * SSM_STATE), F32),
                        pltpu.VMEM((S5_PAIRS, n_states, 2 * SSM_STATE), F32)],
        compiler_params=_cparams("arbitrary"),
        name="s5_scan",
    )(*([u] * S5_BLOCKS), pairs(s0_re), pairs(s0_im), *consts)
    unpair = lambda s: s.transpose(1, 0, 2).reshape(n_states, N_SSM_GROUPS, SSM_STATE)
    return y, unpair(sr), unpair(si)


KEY_NEG_INF = INT_MIN + 0x7FFFFF


def _key_to_float(x):
    return pltpu.bitcast(jnp.where(x < 0, x ^ jnp.int32(0x7FFFFFFF), x), F32)


def _threshold_of(x):
    return jnp.where(x <= KEY_NEG_INF, -jnp.inf, _key_to_float(x))


def _kth_largest(count_ge, shape, n_sel):
    def bit_step(i, x):
        cand = x + lax.shift_left(jnp.int32(1), 31 - i)
        return jnp.where(count_ge(_key_to_float(cand)) >= n_sel, cand, x)
    return _threshold_of(lax.fori_loop(0, 32, bit_step, jnp.full(shape, INT_MIN, jnp.int32)))


def _kth_largest_radix4(count_ge3, shape, n_sel):
    def step(i, x):
        one = lax.shift_left(jnp.int32(1), 30 - 2 * i)
        c1, c2, c3 = x + one, x + 2 * one, x + 3 * one
        n1, n2, n3 = count_ge3(_key_to_float(c1), _key_to_float(c2), _key_to_float(c3))
        return jnp.where(n3 >= n_sel, c3, jnp.where(n2 >= n_sel, c2, jnp.where(n1 >= n_sel, c1, x)))
    return _threshold_of(lax.fori_loop(0, 16, step, jnp.full(shape, INT_MIN, jnp.int32)))


def _rows_sum(x):
    r, c = x.shape
    return jnp.sum(jnp.sum(x.reshape(r // 8, 8, c), axis=0), axis=0, keepdims=True)


def _rows_max(x):
    r, c = x.shape
    return jnp.max(jnp.max(x.reshape(r // 8, 8, c), axis=0), axis=0, keepdims=True)


def _dsa_prompt_kernel(qt_ref, qit_ref, wit_ref, kb_ref, kib_ref, vt_ref, tri_ref, o_ref,
                       key_scr, bias_scr, acc_scr, *, tq, kbs, kba, n_sel):
    qtile = pl.program_id(1)
    n_keys = qtile * tq + tq
    nks = (n_keys + kbs - 1) // kbs
    nka = (n_keys + kba - 1) // kba

    def causal(j, kb):
        kpos = j * kb + lax.broadcasted_iota(jnp.int32, (kb, tq), 0)
        tpos = qtile * tq + lax.broadcasted_iota(jnp.int32, (kb, tq), 1)
        return kpos <= tpos

    qit = qit_ref[...]
    wi_w = jnp.concatenate([qit[h * IDX_DIM:(h + 1) * IDX_DIM] for h in range(N_IDX_HEADS)], axis=1)
    wit = wit_ref[...]

    def score_block(j, _):
        rows = pl.ds(pl.multiple_of(j * kbs, kbs), kbs)
        d = _dot(kib_ref[rows, :], wi_w)
        sc = jnp.zeros((kbs, tq), F32)
        for h in range(N_IDX_HEADS):
            sc += jnp.maximum(d[:, h * tq:(h + 1) * tq], 0.0) * wit[h:h + 1, :]
        key_scr[rows, :] = jnp.where(causal(j, kbs), sc, -jnp.inf)
        return 0
    lax.fori_loop(0, nka * (kba // kbs), score_block, 0)

    def count(pred):
        def body(j, acc):
            blk = key_scr[pl.ds(pl.multiple_of(j * kbs, kbs), kbs), :]
            hit = jnp.where(pred(blk), 1.0, 0.0).astype(BF16)
            return acc + _dot(jnp.ones((8, kbs), BF16), hit)
        acc = lax.fori_loop(0, nks, body, jnp.zeros((8, tq), F32))
        return acc[0:1]

    thr = _kth_largest(lambda cand: count(lambda blk: blk >= cand), (1, tq), n_sel)

    @pl.when(jnp.max(count(lambda blk: blk >= thr)) > n_sel)
    def _():
        room = n_sel - count(lambda blk: blk > thr)
        kbt = tri_ref.shape[0]

        def demote(j, tie_seen):
            rows = pl.ds(pl.multiple_of(j * kbt, kbt), kbt)
            key = key_scr[rows, :]
            tie = jnp.where(key == thr, 1.0, 0.0)
            rank = tie_seen + _dot(tri_ref[...], tie.astype(BF16))
            key_scr[rows, :] = jnp.where((tie > 0.0) & (rank > room), -jnp.inf, key)
            return tie_seen + _rows_sum(tie)
        lax.fori_loop(0, (n_keys + kbt - 1) // kbt, demote, jnp.zeros((1, tq), F32))

    qt = qt_ref[...]
    zero = jnp.zeros((HEAD_DIM, tq), BF16)
    w_qk = []
    for n in range(N_KV_HEADS):
        cols = []
        for g in range(2):
            h = 2 * n + g
            qh = qt[h * HEAD_DIM:(h + 1) * HEAD_DIM]
            cols.append(jnp.concatenate([qh, zero] if n % 2 == 0 else [zero, qh], axis=0))
        w_qk.append(jnp.concatenate(cols, axis=1))

    sub = min(128, kba)

    def scores(kblk, bias, n, i):
        r = slice(i * sub, (i + 1) * sub)
        s = _dot(kblk[r, (n // 2) * LANES:(n // 2 + 1) * LANES], w_qk[n])
        return s + jnp.concatenate([bias[r], bias[r]], axis=1)

    def max_block(j, carry):
        rows = pl.ds(pl.multiple_of(j * kba, kba), kba)
        bias = jnp.where((key_scr[rows, :] >= thr) & causal(j, kba), 0.0, NEG)
        bias_scr[rows, :] = bias
        kblk = kb_ref[rows, :]
        out = []
        for n in range(N_KV_HEADS):
            mx = carry[n]
            for i in range(kba // sub):
                s = scores(kblk, bias, n, i)
                mx = jnp.maximum(mx, jnp.max(s.reshape(sub // 8, 8, 2 * tq), axis=0))
            out.append(mx)
        return tuple(out)
    mx = lax.fori_loop(0, nka, max_block, (jnp.full((8, 2 * tq), NEG, F32),) * N_KV_HEADS)
    m_row = [jnp.max(m, axis=0, keepdims=True) for m in mx]

    acc_scr[...] = jnp.zeros(acc_scr.shape, F32)

    def attend_block(j, carry):
        rows = pl.ds(pl.multiple_of(j * kba, kba), kba)
        bias = bias_scr[rows, :]
        kblk = kb_ref[rows, :]
        out = []
        for n in range(N_KV_HEADS):
            ls = carry[n]
            pieces = []
            for i in range(kba // sub):
                p = jnp.exp(scores(kblk, bias, n, i) - m_row[n])
                ls = ls + jnp.sum(p.reshape(sub // 8, 8, 2 * tq), axis=0)
                pieces.append(p.astype(BF16))
            acc_scr[n] += _dot(vt_ref[n * HEAD_DIM:(n + 1) * HEAD_DIM, rows], jnp.concatenate(pieces, axis=0))
            out.append(ls)
        return tuple(out)
    ls = lax.fori_loop(0, nka, attend_block, (jnp.zeros((8, 2 * tq), F32),) * N_KV_HEADS)

    parts = []
    for n in range(N_KV_HEADS):
        o = acc_scr[n] / jnp.sum(ls[n], axis=0, keepdims=True)
        parts += [o[:, :tq], o[:, tq:]]
    o_ref[...] = jnp.concatenate(parts, axis=0).T


def _upper_tri(n):
    r = np.arange(n)
    return jnp.asarray(r[:, None] <= r[None, :], BF16)


def _lower_tri(n):
    r = np.arange(n)
    return jnp.asarray(r[:, None] >= r[None, :], BF16)


def _dsa_prompt(qt, qit, wit, kb, kib, vt, b, s_len, tq=256, kbs=256, kba=512, kbt=256):
    kbs, kba, kbt = min(kbs, s_len), min(kba, s_len), min(kbt, s_len)
    n_sel = min(TOPK_KEYS, s_len // 4)
    nq = s_len // tq
    qspec = lambda r: pl.BlockSpec((r, tq), lambda bi, qi_: (0, bi * nq + qi_))
    kspec = lambda w: pl.BlockSpec((s_len, w), lambda bi, qi_: (bi, 0))
    return pl.pallas_call(
        functools.partial(_dsa_prompt_kernel, tq=tq, kbs=kbs, kba=kba, n_sel=n_sel),
        grid=(b, nq),
        in_specs=[qspec(512), qspec(256), qspec(8), kspec(256), kspec(IDX_DIM),
                  pl.BlockSpec((256, s_len), lambda bi, qi_: (0, bi)),
                  pl.BlockSpec((kbt, kbt), lambda bi, qi_: (0, 0))],
        out_specs=pl.BlockSpec((tq, D_ATT), lambda bi, qi_: (bi * nq + qi_, 0)),
        out_shape=jax.ShapeDtypeStruct((b * s_len, D_ATT), F32),
        scratch_shapes=[pltpu.VMEM((s_len, tq), F32),
                        pltpu.VMEM((s_len, tq), F32),
                        pltpu.VMEM((N_KV_HEADS, HEAD_DIM, 2 * tq), F32)],
        compiler_params=_cparams("arbitrary", "arbitrary"),
        name="dsa_prompt",
    )(qt, qit, wit, kb, kib, vt, _lower_tri(kbt))


def _dsa_sample_kernel(pt_ref, q_ref, qis_ref, wi_ref, knew_ref, vnew_ref, kinew_ref, tri_ref,
                       ck_ref, cv_ref, cki_ref, o_ref,
                       kibuf, kbuf, vbuf, sc_scr, sem_ki, sem_k, sem_v,
                       *, n_pages, n_sel, kbi, kba, ts):
    b = pl.program_id(0)
    nb = pl.num_programs(0)
    slot = b % 2
    past = n_pages * PAGE_SIZE
    rows = N_HEADS * ts
    hrows = rows // N_KV_HEADS

    def page_keys(p):
        return pl.ds(pl.multiple_of(p * PAGE_SIZE, PAGE_SIZE), PAGE_SIZE)

    def ki_copy(bb, sl, p):
        return pltpu.make_async_copy(cki_ref.at[pt_ref[bb, p]], kibuf.at[sl, :, page_keys(p)], sem_ki.at[sl])

    def k_copy(bb, sl, p):
        return pltpu.make_async_copy(ck_ref.at[pt_ref[bb, p]], kbuf.at[sl, :, :, page_keys(p)], sem_k.at[sl])

    def v_copy(bb, sl, p):
        return pltpu.make_async_copy(cv_ref.at[pt_ref[bb, p]], vbuf.at[sl, :, :, page_keys(p)], sem_v.at[sl])

    def for_pages(fn):
        def body(p, _):
            fn(p)
            return 0
        lax.fori_loop(0, n_pages, body, 0)

    def start_all(bb, sl):
        def one(p):
            ki_copy(bb, sl, p).start()
            k_copy(bb, sl, p).start()
            v_copy(bb, sl, p).start()
        for_pages(one)

    @pl.when(b == 0)
    def _():
        start_all(b, slot)

    @pl.when(b + 1 < nb)
    def _():
        start_all(b + 1, 1 - slot)

    for_pages(lambda p: ki_copy(b, slot, p).wait())

    qis = qis_ref[...]
    wi = wi_ref[...]
    n_tiles = past // LANES + 1

    def index_scores(dots):
        d = jnp.maximum(dots, 0.0)
        sc = jnp.zeros((ts, dots.shape[1]), F32)
        for h in range(N_IDX_HEADS):
            sc += d[h * ts:(h + 1) * ts] * wi[:, h:h + 1]
        return sc

    def score_block(j, _):
        cols = pl.ds(pl.multiple_of(j * kbi, kbi), kbi)
        sc_scr[:, cols] = index_scores(_dot(qis, kibuf[slot, :, cols].astype(BF16)))
        return 0
    lax.fori_loop(0, past // kbi, score_block, 0)
    pad_rows = LANES - ts
    kin = jnp.concatenate([kinew_ref[...], jnp.zeros((pad_rows, IDX_DIM), F32)], axis=0)
    new_ok = (lax.broadcasted_iota(jnp.int32, (ts, LANES), 1)
              <= lax.broadcasted_iota(jnp.int32, (ts, LANES), 0))
    sc_scr[:, past:past + LANES] = jnp.where(new_ok, index_scores(_dot_nt(qis, kin.astype(BF16))), -jnp.inf)

    def counts(preds):
        sc = sc_scr[...]
        parts = [[jnp.zeros((ts, LANES), F32), jnp.zeros((ts, LANES), F32)] for _ in preds]
        for c in range(n_tiles):
            tile = sc[:, c * LANES:(c + 1) * LANES]
            for i, pred in enumerate(preds):
                parts[i][c % 2] = parts[i][c % 2] + jnp.where(pred(tile), 1.0, 0.0)
        return [jnp.sum(a + b_, axis=-1, keepdims=True) for a, b_ in parts]

    thr = _kth_largest_radix4(lambda c1, c2, c3: counts([lambda s: s >= c1, lambda s: s >= c2, lambda s: s >= c3]),
                              (ts, 1), n_sel)
    n_ge, n_gt = counts([lambda s: s >= thr, lambda s: s > thr])

    @pl.when(jnp.max(n_ge) > n_sel)
    def _():
        room = n_sel - n_gt
        kbt = tri_ref.shape[0]

        def demote(cols, tie_seen, tri):
            sc = sc_scr[:, cols]
            tie = jnp.where(sc == thr, 1.0, 0.0)
            rank = tie_seen + _dot(tie.astype(BF16), tri)
            sc_scr[:, cols] = jnp.where((tie > 0.0) & (rank > room), -jnp.inf, sc)
            return tie_seen + jnp.sum(tie, axis=-1, keepdims=True)
        seen = lax.fori_loop(0, past // kbt,
                             lambda j, s: demote(pl.ds(pl.multiple_of(j * kbt, kbt), kbt), s, tri_ref[...]),
                             jnp.zeros((ts, 1), F32))
        demote(pl.ds(past, LANES), seen, tri_ref[:LANES, :LANES])

    sc_scr[...] = jnp.where(sc_scr[...] >= thr, 0.0, NEG)
    sc_scr[:, past:past + LANES] = jnp.where(new_ok, sc_scr[:, past:past + LANES], NEG)

    for_pages(lambda p: k_copy(b, slot, p).wait())
    for_pages(lambda p: v_copy(b, slot, p).wait())

    q = q_ref[...]
    zpad = jnp.zeros((pad_rows, HEAD_DIM), F32)

    def key_block(j):
        return pl.ds(pl.multiple_of(j * kba, kba), kba)

    def head_rows(x, n):
        return x[n * hrows:(n + 1) * hrows]

    def scores(dots_of_head, bias):
        s = jnp.concatenate([dots_of_head(head_rows(q, n), n) for n in range(N_KV_HEADS)], axis=0)
        return s + jnp.concatenate([bias] * N_HEADS, axis=0)

    def block_scores(j):
        return scores(lambda qn, n: _dot(qn, kbuf[slot, n, :, key_block(j)].astype(BF16)), sc_scr[:, key_block(j)])

    def new_scores():
        return scores(lambda qn, n: _dot_nt(qn, jnp.concatenate([knew_ref[n], zpad], axis=0).astype(BF16)),
                      sc_scr[:, past:past + LANES])

    def lane_tiles(x, op, acc):
        for c in range(x.shape[1] // LANES):
            acc = op(acc, x[:, c * LANES:(c + 1) * LANES])
        return acc

    mx = lax.fori_loop(0, past // kba, lambda j, m: lane_tiles(block_scores(j), jnp.maximum, m),
                       jnp.full((rows, LANES), NEG, F32))
    m_row = jnp.max(jnp.maximum(mx, new_scores()), axis=-1, keepdims=True)

    def value_product(p, pv_of_head, accs):
        return tuple(accs[n] + pv_of_head(head_rows(p, n).astype(BF16), n) for n in range(N_KV_HEADS))

    def attend_block(j, carry):
        ls, accs = carry
        p = jnp.exp(block_scores(j) - m_row)
        accs = value_product(p, lambda pn, n: _dot_nt(pn, vbuf[slot, n, :, key_block(j)].astype(BF16)), accs)
        return lane_tiles(p, jnp.add, ls), accs
    zero_acc = (jnp.zeros((hrows, HEAD_DIM), F32),) * N_KV_HEADS
    ls, accs = lax.fori_loop(0, past // kba, attend_block, (jnp.zeros((rows, LANES), F32), zero_acc))
    p_new = jnp.exp(new_scores() - m_row)
    accs = value_product(
        p_new, lambda pn, n: _dot(pn, jnp.concatenate([vnew_ref[n], zpad], axis=0).astype(BF16)), accs)
    l_row = jnp.sum(ls + p_new, axis=-1, keepdims=True)
    o_ref[...] = jnp.concatenate(accs, axis=0) / l_row


def _dsa_sample(q, qi, wi, k, v, ki, cache_k, cache_v, cache_kidx, page_table, ts, kbi=2048, kba=1024, kbt=512):
    bd, n_pages = page_table.shape
    past = n_pages * PAGE_SIZE
    kbi, kba, kbt = min(kbi, past), min(kba, past), min(kbt, past)
    n_sel = min(TOPK_KEYS, (past + ts) // 4)
    rows = N_HEADS * ts
    heads_first = lambda a, nh: a.reshape(bd, ts, nh, -1).transpose(0, 2, 1, 3)
    qh = heads_first(q, N_HEADS).reshape(bd, rows, HEAD_DIM)
    qis = heads_first(qi, N_IDX_HEADS).reshape(bd, N_IDX_HEADS * ts, IDX_DIM)
    per_b = lambda *s: pl.BlockSpec((None,) + s, lambda i, pt: (i,) + (0,) * len(s))
    any_spec = pl.BlockSpec(memory_space=pl.ANY)
    page_buf = pltpu.VMEM((2, N_KV_HEADS, HEAD_DIM, past), F32)
    o = pl.pallas_call(
        functools.partial(_dsa_sample_kernel, n_pages=n_pages, n_sel=n_sel, kbi=kbi, kba=kba, ts=ts),
        grid_spec=pltpu.PrefetchScalarGridSpec(
            num_scalar_prefetch=1,
            grid=(bd,),
            in_specs=[per_b(rows, HEAD_DIM), per_b(N_IDX_HEADS * ts, IDX_DIM), per_b(ts, N_IDX_HEADS),
                      per_b(N_KV_HEADS, ts, HEAD_DIM), per_b(N_KV_HEADS, ts, HEAD_DIM), per_b(ts, IDX_DIM),
                      pl.BlockSpec((kbt, kbt), lambda i, pt: (0, 0)),
                      any_spec, any_spec, any_spec],
            out_specs=per_b(rows, HEAD_DIM),
            scratch_shapes=[pltpu.VMEM((2, IDX_DIM, past), F32), page_buf, page_buf,
                            pltpu.VMEM((ts, past + LANES), F32),
                            pltpu.SemaphoreType.DMA((2,)),
                            pltpu.SemaphoreType.DMA((2,)),
                            pltpu.SemaphoreType.DMA((2,))]),
        out_shape=jax.ShapeDtypeStruct((bd, rows, HEAD_DIM), F32),
        compiler_params=_cparams("arbitrary"),
        name="dsa_sample",
    )(page_table, qh, qis, wi.reshape(bd, ts, N_IDX_HEADS), heads_first(k, N_KV_HEADS),
      heads_first(v, N_KV_HEADS), ki.reshape(bd, ts, IDX_DIM), _upper_tri(kbt),
      cache_k.transpose(0, 2, 3, 1), cache_v.transpose(0, 2, 3, 1), cache_kidx.transpose(0, 2, 1))
    return o.reshape(bd, N_HEADS, ts, HEAD_DIM).transpose(0, 2, 1, 3).reshape(bd * ts, D_ATT)


def _layer_norm(x, g, b):
    mu = jnp.mean(x, axis=-1, keepdims=True)
    xc = x - mu
    var = jnp.mean(xc * xc, axis=-1, keepdims=True)
    return xc * lax.rsqrt(var + LN_EPS) * g + b


def _rms_norm(x, g):
    return x * lax.rsqrt(jnp.mean(x * x, axis=-1, keepdims=True) + LN_EPS) * g


def _mix_kernel(*refs):
    ys_refs, refs = refs[:S5_BLOCKS], refs[S5_BLOCKS:]
    (ya_ref, x_ref, g1_ref, sc2_ref, sh2_ref, wglu_ref, bglu_ref, gs_ref, ga_ref,
     wout_ref, l1g_ref, l1b_ref, wr_ref, br_ref, tri_ref,
     x1_ref, h2_ref, eidx_ref, gate_ref, rank_ref, cnt_ref, run_scr) = refs
    g = jax.nn.gelu(jnp.concatenate([r[...] for r in ys_refs], axis=-1))
    ssm = g * jax.nn.sigmoid(_dot(g.astype(BF16), wglu_ref[...]) + bglu_ref[...])
    cat = jnp.concatenate([_rms_norm(ssm, gs_ref[...]), _rms_norm(ya_ref[...], ga_ref[...])], axis=-1)
    mixed = _dot(cat.astype(BF16), wout_ref[...])
    x1 = _layer_norm(ALPHA * x_ref[...] + g1_ref[...] * mixed, l1g_ref[...], l1b_ref[...])
    x1_ref[...] = x1
    h2f = x1 * (1.0 + sc2_ref[...]) + sh2_ref[...]
    h2_ref[...] = h2f
    h2 = h2f.astype(BF16)
    aff = jax.nn.sigmoid(_dot(h2, wr_ref[...]))
    vals = aff + br_ref[...]
    tm = aff.shape[0]
    lane = lax.broadcasted_iota(jnp.int32, (tm, N_EXPERTS), 1)
    col = lax.broadcasted_iota(jnp.int32, (tm, 8), 1)
    eidx = jnp.zeros((tm, 8), jnp.int32)
    gsel = jnp.zeros((tm, 8), F32)
    chosen = jnp.zeros((tm, N_EXPERTS), F32)
    picks = []
    for j in range(TOP_K_EXPERTS):
        top = jnp.max(vals, axis=-1, keepdims=True)
        idx = jnp.min(jnp.where(vals == top, lane, N_EXPERTS), axis=-1, keepdims=True)
        hit = lane == idx
        gj = jnp.sum(jnp.where(hit, aff, 0.0), axis=-1, keepdims=True)
        eidx = jnp.where(col == j, idx, eidx)
        gsel = jnp.where(col == j, gj, gsel)
        vals = jnp.where(hit, -jnp.inf, vals)
        chosen = jnp.where(hit, 1.0, chosen)
        picks.append(idx)
    eidx_ref[...] = eidx
    gate_ref[...] = gsel / jnp.sum(gsel, axis=-1, keepdims=True) * ROUTED_SCALE

    @pl.when(pl.program_id(0) == 0)
    def _():
        run_scr[...] = jnp.zeros(run_scr.shape, F32)
    before = run_scr[...] + _dot(tri_ref[...], chosen.astype(BF16))
    rank = jnp.zeros((tm, 8), jnp.int32)
    for j in range(TOP_K_EXPERTS):
        rj = jnp.sum(jnp.where(lane == picks[j], before, 0.0), axis=-1, keepdims=True)
        rank = jnp.where(col == j, rj.astype(jnp.int32), rank)
    rank_ref[...] = rank
    run_scr[...] += jnp.sum(chosen, axis=0, keepdims=True)
    cnt_ref[...] = jnp.broadcast_to(run_scr[...], cnt_ref.shape)


def _mixer_out(ys, ya, x, g1, sc2, sh2, w, rows_per_mod, tm):
    t = x.shape[0]
    ms = _mod_spec(g1, tm, rows_per_mod)
    row = lambda wd: pl.BlockSpec((tm, wd), lambda i: (i, 0))
    full = lambda a: pl.BlockSpec(a.shape, lambda i: (0, 0))
    r = np.arange(tm)
    strict_lower = jnp.asarray(r[None, :] < r[:, None], BF16)
    consts = [w['w_glu'], w['b_glu'], w['g_ssm_out'], w['g_att_out'], w['w_out'], w['ln1_g'], w['ln1_b'],
              w['w_router'], w['b_router'], strict_lower]
    return pl.pallas_call(
        _mix_kernel,
        grid=(t // tm,),
        in_specs=[row(LANES)] * S5_BLOCKS + [row(D_ATT), row(D_MODEL), ms, ms, ms]
                 + [full(a) for a in consts],
        out_specs=[row(D_MODEL), row(D_MODEL), row(8), row(8), row(8),
                   pl.BlockSpec((8, N_EXPERTS), lambda i: (0, 0))],
        out_shape=[jax.ShapeDtypeStruct((t, D_MODEL), F32), jax.ShapeDtypeStruct((t, D_MODEL), F32),
                   jax.ShapeDtypeStruct((t, 8), jnp.int32), jax.ShapeDtypeStruct((t, 8), F32),
                   jax.ShapeDtypeStruct((t, 8), jnp.int32), jax.ShapeDtypeStruct((8, N_EXPERTS), F32)],
        scratch_shapes=[pltpu.VMEM((1, N_EXPERTS), F32)],
        compiler_params=_cparams("arbitrary"),
        name="mixer_out",
    )(*ys, ya, x, g1, sc2, sh2, *consts)


MOE_ROWS = 512


def _experts_kernel(be_ref, nused_ref, x_ref, w1_ref, w3_ref, w2_ref, y_ref):
    i = pl.program_id(0)

    @pl.when(i < nused_ref[0])
    def _():
        x = x_ref[...].astype(BF16)
        h1 = _dot(x, w1_ref[...].astype(BF16))
        h3 = _dot(x, w3_ref[...].astype(BF16))
        a = (h1 * jax.nn.sigmoid(h1) * h3).astype(BF16)
        y_ref[...] = _dot(a, w2_ref[...].astype(BF16)).astype(y_ref.dtype)

    @pl.when(i >= nused_ref[0])
    def _():
        y_ref[...] = jnp.zeros(y_ref.shape, y_ref.dtype)


def _experts(xs, block_e, n_used, w_e1, w_e3, w_e2):
    n_slots = xs.shape[0]
    n_blocks = n_slots // MOE_ROWS
    wspec = lambda r, c: pl.BlockSpec((None, r, c), lambda i, be, nu: (be[i], 0, 0))
    return pl.pallas_call(
        _experts_kernel,
        grid_spec=pltpu.PrefetchScalarGridSpec(
            num_scalar_prefetch=2,
            grid=(n_blocks,),
            in_specs=[pl.BlockSpec((MOE_ROWS, D_MODEL), lambda i, be, nu: (i, 0)),
                      wspec(D_MODEL, D_EXPERT), wspec(D_MODEL, D_EXPERT), wspec(D_EXPERT, D_MODEL)],
            out_specs=pl.BlockSpec((MOE_ROWS, D_MODEL), lambda i, be, nu: (i, 0))),
        out_shape=jax.ShapeDtypeStruct((n_slots, D_MODEL), F32),
        compiler_params=_cparams("arbitrary"),
        name="routed_experts",
    )(block_e, n_used, xs, w_e1, w_e3, w_e2)


def _dispatch_plan(eidx, rank, counts, n_tok):
    padded = (counts + MOE_ROWS - 1) // MOE_ROWS * MOE_ROWS
    pends = jnp.cumsum(padded)
    pstarts = pends - padded
    slot = pstarts[eidx] + rank
    n_blocks = -(-n_tok * TOP_K_EXPERTS // MOE_ROWS) + N_EXPERTS
    first_row = jnp.arange(n_blocks, dtype=jnp.int32) * MOE_ROWS
    block_e = jnp.sum((pends[None, :] <= first_row[:, None]).astype(jnp.int32), axis=1)
    block_e = jnp.minimum(block_e, N_EXPERTS - 1)
    n_used = (pends[-1:] // MOE_ROWS).astype(jnp.int32)
    return slot, block_e, n_used, n_blocks * MOE_ROWS


def _final_kernel(*refs):
    ye_refs = refs[:TOP_K_EXPERTS]
    gate_ref, h2_ref, x1_ref, g2_ref, ws1_ref, ws3_ref, ws2_ref, l2g_ref, l2b_ref, o_ref = refs[TOP_K_EXPERTS:]
    h2 = h2_ref[...].astype(BF16)
    h1 = _dot(h2, ws1_ref[...])
    h3 = _dot(h2, ws3_ref[...])
    a = (h1 * jax.nn.sigmoid(h1) * h3).astype(BF16)
    gate = gate_ref[...]
    routed = ye_refs[0][...].astype(F32) * gate[:, 0:1]
    for j in range(1, TOP_K_EXPERTS):
        routed += ye_refs[j][...].astype(F32) * gate[:, j:j + 1]
    ffn = routed + _dot(a, ws2_ref[...])
    o_ref[...] = _layer_norm(ALPHA * x1_ref[...] + g2_ref[...] * ffn, l2g_ref[...], l2b_ref[...])


def _final(ye_parts, gates, h2, x1, g2, w, rows_per_mod, tm):
    t = x1.shape[0]
    row = lambda wd: pl.BlockSpec((tm, wd), lambda i: (i, 0))
    full = lambda a: pl.BlockSpec(a.shape, lambda i: (0, 0))
    consts = [w['w_s1'], w['w_s3'], w['w_s2'], w['ln2_g'], w['ln2_b']]
    return pl.pallas_call(
        _final_kernel,
        grid=(t // tm,),
        in_specs=[row(D_MODEL)] * TOP_K_EXPERTS
                 + [row(8), row(D_MODEL), row(D_MODEL), _mod_spec(g2, tm, rows_per_mod)]
                 + [full(a) for a in consts],
        out_specs=row(D_MODEL),
        out_shape=jax.ShapeDtypeStruct((t, D_MODEL), F32),
        compiler_params=_cparams("arbitrary"),
        name="final_residual",
    )(*ye_parts, gates, h2, x1, g2, *consts)


def kernel(x_prompt, x_sample, c_prompt, c_sample, cache_k, cache_v, cache_kidx, state_ssm_re, state_ssm_im, page_table, w_ada, b_ada, w_in, ssm_a_re, ssm_a_im, ssm_log_dt, ssm_b_re, ssm_b_im, ssm_c_re, ssm_c_im, ssm_d, w_glu, b_glu, g_ssm_out, g_att_out, w_out, ln1_g, ln1_b, w_router, b_router, w_e1, w_e3, w_e2, w_s1, w_s3, w_s2, ln2_g, ln2_b):
    assert w_in.shape[0] == 1, "one layer"
    b, s, d = x_prompt.shape
    bd, ts, _ = x_sample.shape
    tp, tsn = b * s, bd * ts
    tm = 512
    row = lambda a: a[0].reshape(1, -1)
    w = dict(w_glu=w_glu[0].astype(BF16), b_glu=row(b_glu), g_ssm_out=row(g_ssm_out), g_att_out=row(g_att_out),
             w_out=w_out[0].astype(BF16), ln1_g=row(ln1_g), ln1_b=row(ln1_b),
             w_router=w_router[0].astype(BF16), b_router=row(b_router),
             w_s1=w_s1[0].astype(BF16), w_s3=w_s3[0].astype(BF16), w_s2=w_s2[0].astype(BF16),
             ln2_g=row(ln2_g), ln2_b=row(ln2_b))
    w_sample, w_prompt, w_prompt_t = _projection_weights(w_in[0])
    ssm_args = (ssm_a_re[0], ssm_a_im[0], ssm_log_dt[0], ssm_b_re[0], ssm_b_im[0], ssm_c_re[0], ssm_c_im[0],
                ssm_d[0])

    mod = _ada_mod(jnp.concatenate([c_prompt, c_sample], axis=0), w_ada[0], b_ada[0])
    mod_p = mod[:b].reshape(b, 6, 1, d)
    mod_s = jnp.broadcast_to(mod[b:].reshape(bd, 1, 6, d), (bd, ts, 6, d)).reshape(tsn, 6, d)

    xp = x_prompt.reshape(tp, d)
    u, kb, kib, qt, qit, vt, wit, kt_p, vt_p, kit_p = _in_projection_prompt(
        xp, mod_p[:, 1], mod_p[:, 0], w_prompt, w_prompt_t, s, tm)
    mats = _ssm_matrices(*ssm_args)
    zero_state = jnp.zeros((b, N_SSM_GROUPS, SSM_STATE), F32)
    ys_p, re_p, im_p = _s5_scan(u.reshape(b, s, D_SSM), zero_state, zero_state, mats, True, min(s, 128))
    ya_p = _dsa_prompt(qt, qit, wit, kb, kib, vt, b, s)
    x1_p, h2_p, eidx_p, gate_p, rank_p, cnt_p = _mixer_out([y.reshape(tp, LANES) for y in ys_p], ya_p, xp,
                                                           mod_p[:, 2], mod_p[:, 4], mod_p[:, 3], w, s, tm)

    assert ts == S5_CHUNK, "each sample sequence is one S5 chunk"
    xs = x_sample.reshape(tsn, d)
    tms = min(tm, tsn)
    u, q, k_s, v_s, qi, ki_s, wi = _in_projection(xs, mod_s[:, 1], mod_s[:, 0], w_sample, None, tms)
    ys_s, re_s, im_s = _s5_scan(u.reshape(1, tsn, D_SSM), state_ssm_re[0], state_ssm_im[0], mats, False, tsn)
    ya_s = _dsa_sample(q, qi, wi, k_s, v_s, ki_s, cache_k[0], cache_v[0], cache_kidx[0], page_table, ts)
    x1_s, h2_s, eidx_s, gate_s, rank_s, cnt_s = _mixer_out([y.reshape(tsn, LANES) for y in ys_s], ya_s, xs,
                                                           mod_s[:, 2], mod_s[:, 4], mod_s[:, 3], w, None, tms)

    n_tok = tp + tsn
    cnt_p, cnt_s = cnt_p[0].astype(jnp.int32), cnt_s[0].astype(jnp.int32)
    eidx_p, eidx_s = eidx_p[:, :TOP_K_EXPERTS], eidx_s[:, :TOP_K_EXPERTS]
    eidx = jnp.concatenate([eidx_p, eidx_s], axis=0)
    rank = jnp.concatenate([rank_p[:, :TOP_K_EXPERTS], rank_s[:, :TOP_K_EXPERTS] + cnt_p[eidx_s]], axis=0)
    slot, block_e, n_used, n_slots = _dispatch_plan(eidx, rank, cnt_p + cnt_s, n_tok)
    tok = jnp.broadcast_to(jnp.arange(n_tok, dtype=jnp.int32)[:, None], slot.shape)
    slot_tok = jnp.zeros((n_slots,), jnp.int32).at[slot.reshape(-1)].set(tok.reshape(-1))
    h2 = jnp.concatenate([h2_p, h2_s], axis=0)
    ye = _experts(h2[slot_tok], block_e, n_used, w_e1[0], w_e3[0], w_e2[0])
    parts_p = [ye[slot[:tp, j]] for j in range(TOP_K_EXPERTS)]
    parts_s = [ye[slot[tp:, j]] for j in range(TOP_K_EXPERTS)]

    y_p = _final(parts_p, gate_p, h2_p, x1_p, mod_p[:, 5], w, s, tm)
    y_s = _final(parts_s, gate_s, h2_s, x1_s, mod_s[:, 5], w, None, tms)

    heads_last = lambda a: a.reshape(b, N_KV_HEADS, HEAD_DIM, s).transpose(0, 3, 1, 2)[None]
    kvd = (1, bd, ts, N_KV_HEADS, HEAD_DIM)
    return (y_p.reshape(b, s, d), y_s.reshape(bd, ts, d),
            heads_last(kt_p), heads_last(vt_p), kit_p.transpose(0, 2, 1)[None], re_p[None], im_p[None],
            k_s.reshape(kvd), v_s.reshape(kvd), ki_s.reshape(1, bd, ts, IDX_DIM), re_s[None], im_s[None])
```
